```python
import jax, jax.numpy as jnp
from jax import lax
import numpy as np

D_MODEL = 1024
BATCH = 4
SEQ = 4096
DEPTH = 4

CHUNK = 64
N_MEM = 256
BRANCH_WIDTH = 512
N_BRANCH = 3
LRU_WIDTH = BRANCH_WIDTH
LRU_BLOCKS = 8
LRU_BLOCK = LRU_WIDTH // LRU_BLOCKS
CONV_WIDTH = 4
LRU_C = 8.0
POOL_WIDTH = BRANCH_WIDTH
POOL_WINDOWS = (2, 4, 8, 16)
POOL_GROUPS = len(POOL_WINDOWS)
POOL_GROUP = POOL_WIDTH // POOL_GROUPS
ATT_HEADS = 8
ATT_HEAD_DIM = 64
ATT_WIDTH = ATT_HEADS * ATT_HEAD_DIM
IDX_HEADS = 8
IDX_DIM = 64
MAX_TOPK = 256
Q_BLOCK = 128
MEM_HEADS = 4
MEM_HEAD_DIM = 128
MEM_WIDTH = MEM_HEADS * MEM_HEAD_DIM
D_FF = ((8 * D_MODEL // 3 + 255) // 256) * 256
RMS_EPS = 1e-6

IN_SIZES = (LRU_WIDTH,
            LRU_WIDTH,
            POOL_WIDTH,
            ATT_WIDTH,
            ATT_HEAD_DIM,
            ATT_HEAD_DIM,
            IDX_HEADS * IDX_DIM,
            IDX_DIM,
            IDX_HEADS,
            N_BRANCH * D_MODEL)
D_IN = sum(IN_SIZES)

kernel_name = 'hybrid_rglru_pool_dsa_streaming_block'


def rms_norm(x, g):
    xf = x.astype(jnp.float32)
    y = xf * lax.rsqrt(jnp.mean(xf * xf, axis=-1, keepdims=True) + RMS_EPS)
    return (y * g.astype(jnp.float32)).astype(x.dtype)


def split_columns(h, sizes):
    parts, start = [], 0
    for n in sizes:
        parts.append(h[..., start:start + n])
        start += n
    return parts


def causal_depthwise_conv(x, w, b):
    c = x.shape[-1]
    y = lax.conv_general_dilated(x, w[:, None, :].astype(x.dtype), window_strides=(1,),
                                 padding=[(w.shape[0] - 1, 0)],
                                 dimension_numbers=('NWC', 'WIO', 'NWC'),
                                 feature_group_count=c)
    return y + b


def rg_lru(x, w_a, b_a, w_x, b_x, a_param):
    bsz, s, _ = x.shape
    xb = x.reshape(bsz, s, LRU_BLOCKS, LRU_BLOCK)
    r = jax.nn.sigmoid((jnp.einsum('bshi,hij->bshj', xb, w_a).reshape(bsz, s, LRU_WIDTH) + b_a).astype(jnp.float32))
    i = jax.nn.sigmoid((jnp.einsum('bshi,hij->bshj', xb, w_x).reshape(bsz, s, LRU_WIDTH) + b_x).astype(jnp.float32))
    log_a = -LRU_C * r * jax.nn.softplus(-a_param.astype(jnp.float32))
    a = jnp.exp(log_a)
    mult = jnp.sqrt(jnp.maximum(1.0 - jnp.exp(2.0 * log_a), 0.0))
    u = x.astype(jnp.float32) * i * mult

    def combine(left, right):
        return left[0] * right[0], right[0] * left[1] + right[1]

    _, h = lax.associative_scan(combine, (a, u), axis=1)
    return h.astype(x.dtype)


def multiscale_pool(p, w_pool, pool_scale):
    bsz, s, _ = p.shape
    pf = p.astype(jnp.float32).reshape(bsz, s, POOL_GROUPS, POOL_GROUP)
    cs = jnp.pad(jnp.cumsum(pf, axis=1), ((0, 0), (1, 0), (0, 0), (0, 0)))
    t = jnp.arange(s)
    means = []
    for g, w in enumerate(POOL_WINDOWS):
        start = jnp.maximum(t + 1 - w, 0)
        win_sum = cs[:, 1:, g] - cs[:, start, g]
        count = (t + 1 - start).astype(jnp.float32)[None, :, None]
        means.append(win_sum / count)
    pooled = jnp.stack(means, axis=2) - pf
    mixed = jnp.einsum('bsgc,gcd->bsgd', pooled, w_pool.astype(jnp.float32)).reshape(bsz, s, POOL_WIDTH)
    return (mixed * pool_scale.astype(jnp.float32)).astype(p.dtype)


def dsa_attention(q, k, v, q_idx, k_idx, w_idx):
    bsz, s = q.shape[:2]
    top_k = min(MAX_TOPK, s // 4)
    nb = s // Q_BLOCK
    key_chunk = jnp.arange(s) // CHUNK
    k_idx_f = k_idx.astype(jnp.float32)

    def to_blocks(a):
        return jnp.moveaxis(a.reshape(bsz, nb, Q_BLOCK, *a.shape[2:]), 1, 0)

    def one_block(args):
        qb, qib, wb, pos = args
        q_chunk = pos // CHUNK
        admissible = key_chunk[None, :] <= q_chunk[:, None]
        logits = jnp.einsum('bqhd,bsd->bqhs', qib.astype(jnp.float32), k_idx_f) * (IDX_DIM ** -0.5)
        score = jnp.einsum('bqh,bqhs->bqs', wb.astype(jnp.float32) * (IDX_HEADS ** -0.5), jax.nn.relu(logits))
        score = jnp.where(admissible[None], score, -jnp.inf)
        _, sel = lax.top_k(score, top_k)
        kg = jax.vmap(lambda kk, ii: kk[ii])(k, sel)
        vg = jax.vmap(lambda vv, ii: vv[ii])(v, sel)
        valid = key_chunk[sel] <= q_chunk[None, :, None]
        att = jnp.einsum('bqhd,bqkd->bqhk', qb, kg).astype(jnp.float32) * (ATT_HEAD_DIM ** -0.5)
        att = jnp.where(valid[:, :, None, :], att, -jnp.inf)
        prob = jax.nn.softmax(att, axis=-1).astype(vg.dtype)
        return jnp.einsum('bqhk,bqkd->bqhd', prob, vg)

    q_pos = jnp.arange(s).reshape(nb, Q_BLOCK)
    out = lax.map(one_block, (to_blocks(q), to_blocks(q_idx), to_blocks(w_idx), q_pos))
    return jnp.moveaxis(out, 0, 1).reshape(bsz, s, ATT_WIDTH)


def memory_cross_attention(h, m, w_q, w_kv, w_o):
    bsz, s, _ = h.shape
    q = (h @ w_q).reshape(bsz, s, MEM_HEADS, MEM_HEAD_DIM)
    k, v = jnp.split(m @ w_kv, 2, axis=-1)
    k = k.reshape(bsz, -1, MEM_HEADS, MEM_HEAD_DIM)
    v = v.reshape(bsz, -1, MEM_HEADS, MEM_HEAD_DIM)
    att = jnp.einsum('bshd,bmhd->bhsm', q, k).astype(jnp.float32) * (MEM_HEAD_DIM ** -0.5)
    prob = jax.nn.softmax(att, axis=-1).astype(v.dtype)
    o = jnp.einsum('bhsm,bmhd->bshd', prob, v).reshape(bsz, s, MEM_WIDTH)
    return o @ w_o


def setup_inputs(seed: int = 0) -> dict:
    key = jax.random.key(seed)
    ks = iter(jax.random.split(key, 40))

    def nrm(shape, scale):
        return jax.random.normal(next(ks), shape, jnp.float32) * scale

    def gain(shape):
        return 1.0 + nrm(shape, 0.02)

    L = DEPTH
    u = jax.random.uniform(next(ks), (L, LRU_WIDTH), jnp.float32, 0.9, 0.999)
    a0 = u ** (1.0 / LRU_C)
    lru_a_param = jnp.log(a0) - jnp.log1p(-a0)
    return {
        'x': nrm((BATCH, SEQ, D_MODEL), 1.0),
        'mem': nrm((BATCH, N_MEM, D_MODEL), 1.0),
        'g_mix_pre': gain((L, D_MODEL)),
        'w_in': nrm((L, D_MODEL, D_IN), D_MODEL ** -0.5),
        'conv_w': nrm((L, CONV_WIDTH, LRU_WIDTH), CONV_WIDTH ** -0.5),
        'conv_b': nrm((L, LRU_WIDTH), 0.02),
        'lru_w_a': nrm((L, LRU_BLOCKS, LRU_BLOCK, LRU_BLOCK), LRU_BLOCK ** -0.5),
        'lru_b_a': nrm((L, LRU_WIDTH), 0.02),
        'lru_w_x': nrm((L, LRU_BLOCKS, LRU_BLOCK, LRU_BLOCK), LRU_BLOCK ** -0.5),
        'lru_b_x': nrm((L, LRU_WIDTH), 0.02),
        'lru_a_param': lru_a_param,
        'w_pool': nrm((L, POOL_GROUPS, POOL_GROUP, POOL_GROUP), POOL_GROUP ** -0.5),
        'pool_scale': gain((L, POOL_WIDTH)),
        'w_branch': nrm((L, N_BRANCH, BRANCH_WIDTH, D_MODEL), BRANCH_WIDTH ** -0.5),
        'b_gate': nrm((L, N_BRANCH, D_MODEL), 0.02),
        'w_out': nrm((L, D_MODEL, D_MODEL), D_MODEL ** -0.5),
        'g_mix_post': gain((L, D_MODEL)),
        'g_mem_pre': gain((L, D_MODEL)),
        'g_mem_kv': gain((L, D_MODEL)),
        'w_mem_q': nrm((L, D_MODEL, MEM_WIDTH), D_MODEL ** -0.5),
        'w_mem_kv': nrm((L, D_MODEL, 2 * MEM_WIDTH), D_MODEL ** -0.5),
        'w_mem_o': nrm((L, MEM_WIDTH, D_MODEL), MEM_WIDTH ** -0.5),
        'g_mem_post': gain((L, D_MODEL)),
        'g_ffn_pre': gain((L, D_MODEL)),
        'w_ffn_in': nrm((L, D_MODEL, 2 * D_FF), D_MODEL ** -0.5),
        'w_ffn_out': nrm((L, D_FF, D_MODEL), D_FF ** -0.5),
        'g_ffn_post': gain((L, D_MODEL)),
    }


def reference(x, mem, g_mix_pre, w_in, conv_w, conv_b, lru_w_a, lru_b_a, lru_w_x, lru_b_x,
              lru_a_param, w_pool, pool_scale, w_branch, b_gate, w_out, g_mix_post,
              g_mem_pre, g_mem_kv, w_mem_q, w_mem_kv, w_mem_o, g_mem_post,
              g_ffn_pre, w_ffn_in, w_ffn_out, g_ffn_post):
    bsz, s, _ = x.shape
    for l in range(DEPTH):
        h = rms_norm(x, g_mix_pre[l])
        (lru_x, lru_gate, pool_in, q, k, v, q_idx, k_idx, w_idx,
         gate_logits) = split_columns(h @ w_in[l], IN_SIZES)
        ua = causal_depthwise_conv(lru_x, conv_w[l], conv_b[l])
        ya = rg_lru(ua, lru_w_a[l], lru_b_a[l], lru_w_x[l], lru_b_x[l], lru_a_param[l]) * jax.nn.gelu(lru_gate)
        yb = multiscale_pool(pool_in, w_pool[l], pool_scale[l])
        yc = dsa_attention(q.reshape(bsz, s, ATT_HEADS, ATT_HEAD_DIM), k, v,
                           q_idx.reshape(bsz, s, IDX_HEADS, IDX_DIM), k_idx, w_idx)
        up = jnp.einsum('bsnc,ncd->bsnd', jnp.stack([ya, yb, yc], axis=2), w_branch[l])
        gates = jax.nn.sigmoid(gate_logits.reshape(bsz, s, N_BRANCH, D_MODEL) + b_gate[l])
        merged = jnp.sum(gates * up, axis=2)
        x = x + rms_norm(merged @ w_out[l], g_mix_post[l])
        h = rms_norm(x, g_mem_pre[l])
        m = rms_norm(mem, g_mem_kv[l])
        x = x + rms_norm(memory_cross_attention(h, m, w_mem_q[l], w_mem_kv[l], w_mem_o[l]), g_mem_post[l])
        h = rms_norm(x, g_ffn_pre[l])
        f_gate, f_up = jnp.split(h @ w_ffn_in[l], 2, axis=-1)
        x = x + rms_norm((jax.nn.silu(f_gate) * f_up) @ w_ffn_out[l], g_ffn_post[l])
    return x
```

```python
import functools

import jax
import jax.numpy as jnp
from jax import lax
from jax.experimental import pallas as pl
from jax.experimental.pallas import tpu as pltpu

F32 = jnp.float32
BF16 = jnp.bfloat16

D_MODEL = 1024
CHUNK = 64
CHUNK_SHIFT = 6
N_MEM = 256
BRANCH_WIDTH = 512
N_BRANCH = 3
LRU_BLOCKS = 8
CONV_WIDTH = 4
LRU_C = 8.0
POOL_WINDOWS = (2, 4, 8, 16)
POOL_GROUP = BRANCH_WIDTH // len(POOL_WINDOWS)
ATT_HEADS = 8
HEAD_DIM = 64
IDX_HEADS = 8
IDX_DIM = 64
MAX_TOPK = 256
MEM_HEADS = 4
MEM_HEAD_DIM = 128
MEM_WIDTH = MEM_HEADS * MEM_HEAD_DIM
D_FF = 2816
RMS_EPS = 1e-6

LANES = 128
VMEM_LIMIT = 56 * 1024 * 1024

C_LX, C_LG, C_PIN, C_Q = 0, 512, 1024, 1536
C_KV = 2048
C_QI = 2176
C_KI = 2688
C_WI = 2816
C_GL = 2944
D_PACK = C_GL + N_BRANCH * D_MODEL


def _params(sem, vmem=VMEM_LIMIT):
    return pltpu.CompilerParams(dimension_semantics=sem, vmem_limit_bytes=vmem)


def _resident(shape):
    nd = len(shape)
    return pl.BlockSpec(shape, lambda *_: (0,) * nd, pipeline_mode=pl.Buffered(1))


def _rms(x, g):
    return x * lax.rsqrt(jnp.mean(x * x, axis=-1, keepdims=True) + RMS_EPS) * g


def _mix_in_kernel(x_ref, g_ref, w_ref, lx_ref, lg_ref, pin_ref, q_ref, kv_ref,
                   qi_ref, ki_ref, wi_ref, gl_ref):
    h = _rms(x_ref[...], g_ref[...]).astype(BF16)

    def proj(lo, hi):
        return jnp.dot(h, w_ref[:, lo:hi], preferred_element_type=F32)

    lx_ref[...] = proj(C_LX, C_LG)
    lg_ref[...] = proj(C_LG, C_PIN)
    pin_ref[...] = proj(C_PIN, C_Q)
    q_ref[...] = (proj(C_Q, C_KV) * (HEAD_DIM ** -0.5)).astype(BF16)
    kv_ref[...] = proj(C_KV, C_QI).astype(BF16)
    qi_ref[...] = (proj(C_QI, C_KI) * (IDX_DIM ** -0.5)).astype(BF16)
    ki_ref[...] = proj(C_KI, C_WI).astype(BF16)
    wi_ref[...] = proj(C_WI, C_GL) * (IDX_HEADS ** -0.5)
    gl_ref[...] = proj(C_GL, D_PACK)


def _mix_in(xt, g, w, tm):
    t = xt.shape[0]
    row = lambda n: pl.BlockSpec((tm, n), lambda i: (i, 0))
    outs = [(512, F32), (512, F32), (512, F32), (512, BF16), (128, BF16),
            (512, BF16), (128, BF16), (128, F32), (N_BRANCH * D_MODEL, F32)]
    return pl.pallas_call(
        _mix_in_kernel,
        grid=(t // tm,),
        in_specs=[row(D_MODEL), _resident((1, D_MODEL)), _resident((D_MODEL, D_PACK))],
        out_specs=[row(n) for n, _ in outs],
        out_shape=[jax.ShapeDtypeStruct((t, n), dt) for n, dt in outs],
        compiler_params=_params(("parallel",)),
    )(xt, g, w)


CONV_PAD = 8
POOL_PAD = 16


def _log1p(x):
    u = 1.0 + x
    return jnp.where(u == 1.0, x, jnp.log(u) * (x / (u - 1.0)))


def _softplus(x):
    return jnp.maximum(x, 0.0) + _log1p(jnp.exp(-jnp.abs(x)))


def _lru_pool_kernel(lx_ref, lg_ref, pin_ref, cw_ref, cb_ref, wa_ref, ba_ref, wx_ref,
                     bx_ref, ap_ref, wp_ref, ps_ref, ya_ref, yb_ref,
                     xbuf, pbuf, hc_ref, *, ts):
    s = pl.program_id(1)

    @pl.when(s == 0)
    def _():
        xbuf[0:CONV_PAD, :] = jnp.zeros((CONV_PAD, BRANCH_WIDTH), F32)
        pbuf[0:POOL_PAD, :] = jnp.zeros((POOL_PAD, BRANCH_WIDTH), F32)
        hc_ref[...] = jnp.zeros_like(hc_ref)

    xbuf[CONV_PAD:CONV_PAD + ts, :] = lx_ref[...]
    ua = cb_ref[...] + cw_ref[CONV_WIDTH - 1:CONV_WIDTH, :] * xbuf[CONV_PAD:CONV_PAD + ts, :]
    for j in range(CONV_WIDTH - 1):
        off = CONV_PAD - (CONV_WIDTH - 1) + j
        ua = ua + cw_ref[j:j + 1, :] * xbuf[off:off + ts, :]
    xbuf[0:CONV_PAD, :] = xbuf[ts:ts + CONV_PAD, :]

    ub = ua.astype(BF16)
    r = jax.nn.sigmoid(jnp.dot(ub, wa_ref[...], preferred_element_type=F32) + ba_ref[...])
    ig = jax.nn.sigmoid(jnp.dot(ub, wx_ref[...], preferred_element_type=F32) + bx_ref[...])
    log_a = -LRU_C * r * _softplus(-ap_ref[...])
    a = jnp.exp(log_a)
    mult = jnp.sqrt(jnp.maximum(1.0 - jnp.exp(2.0 * log_a), 0.0))
    u = ua * ig * mult

    row = lax.broadcasted_iota(jnp.int32, (ts, LANES), 0)
    for c in range(BRANCH_WIDTH // LANES):
        cs = slice(c * LANES, (c + 1) * LANES)
        ac, hcur = a[:, cs], u[:, cs]
        sh = 1
        while sh < ts:
            keep = row >= sh
            a_sh = jnp.where(keep, pltpu.roll(ac, sh, 0), 1.0)
            h_sh = jnp.where(keep, pltpu.roll(hcur, sh, 0), 0.0)
            hcur = ac * h_sh + hcur
            ac = ac * a_sh
            sh *= 2
        hfull = hcur + ac * hc_ref[:, cs]
        hc_ref[:, cs] = hfull[ts - 1:ts, :]
        ya_ref[:, cs] = (hfull * jax.nn.gelu(lg_ref[:, cs], approximate=True)).astype(BF16)

    pbuf[POOL_PAD:POOL_PAD + ts, :] = pin_ref[...]
    tpos = s * ts + row
    for gi, win in enumerate(POOL_WINDOWS):
        cs = slice(gi * POOL_GROUP, (gi + 1) * POOL_GROUP)
        cur = pbuf[POOL_PAD:POOL_PAD + ts, cs]
        acc = cur
        for j in range(1, win):
            acc = acc + pbuf[POOL_PAD - j:POOL_PAD - j + ts, cs]
        count = jnp.minimum(tpos + 1, win).astype(F32)
        pooled = (acc / count - cur).astype(BF16)
        mixed = jnp.dot(pooled, wp_ref[gi], preferred_element_type=F32)
        yb_ref[:, cs] = (mixed * ps_ref[:, cs]).astype(BF16)
    pbuf[0:POOL_PAD, :] = pbuf[ts:ts + POOL_PAD, :]


def _lru_pool(lx, lg, pin, cw, cb, wa, ba, wx, bx, ap, wp, ps, bsz, seq, ts):
    nt = seq // ts
    tok = pl.BlockSpec((ts, BRANCH_WIDTH), lambda b, s: (b * nt + s, 0))
    vec = _resident((1, BRANCH_WIDTH))
    return pl.pallas_call(
        functools.partial(_lru_pool_kernel, ts=ts),
        grid=(bsz, nt),
        in_specs=[tok, tok, tok, _resident((CONV_WIDTH, BRANCH_WIDTH)), vec,
                  _resident((BRANCH_WIDTH, BRANCH_WIDTH)), vec,
                  _resident((BRANCH_WIDTH, BRANCH_WIDTH)), vec, vec,
                  _resident((len(POOL_WINDOWS), POOL_GROUP, POOL_GROUP)), vec],
        out_specs=[tok, tok],
        out_shape=[jax.ShapeDtypeStruct((bsz * seq, BRANCH_WIDTH), BF16)] * 2,
        scratch_shapes=[pltpu.VMEM((CONV_PAD + ts, BRANCH_WIDTH), F32),
                        pltpu.VMEM((POOL_PAD + ts, BRANCH_WIDTH), F32),
                        pltpu.VMEM((1, BRANCH_WIDTH), F32)],
        compiler_params=_params(("arbitrary", "arbitrary")),
    )(lx, lg, pin, cw, cb, wa, ba, wx, bx, ap, wp, ps)


def _key_to_float(key):
    bits = key ^ (lax.shift_right_arithmetic(key, 31) & jnp.int32(0x7FFFFFFF))
    return lax.bitcast_convert_type(bits, F32)


def _dsa_kernel(q_ref, qi_ref, wi_ref, kv_ref, ki_ref, o_ref, sc_ref, wb_ref, j_ref,
                *, tq, kc, seq, top_k):
    i = pl.program_id(1)
    nch = ((i + 1) * tq + kc - 1) // kc
    nslab = kc // LANES
    idx_bits = seq.bit_length() - 1

    row = lax.broadcasted_iota(jnp.int32, (tq, LANES), 0)
    lane = lax.broadcasted_iota(jnp.int32, (tq, LANES), 1)
    qpos = i * tq + row
    lim = (lax.shift_right_logical(qpos, CHUNK_SHIFT) + 1) * CHUNK

    w = wi_ref[...]
    for h in range(IDX_HEADS):
        wb_ref[h] = jnp.broadcast_to(w[:, h:h + 1], (tq, LANES))

    def idx_body(c, carry):
        off = pl.multiple_of(c * kc, kc)
        kic = ki_ref[pl.ds(off, kc), :][:, :IDX_DIM]
        acc = [jnp.zeros((tq, LANES), F32) for _ in range(nslab)]
        for h in range(IDX_HEADS):
            lg = lax.dot_general(qi_ref[:, h * IDX_DIM:(h + 1) * IDX_DIM], kic,
                                 (((1,), (1,)), ((), ())), preferred_element_type=F32)
            wb = wb_ref[h]
            for j in range(nslab):
                acc[j] = acc[j] + jnp.maximum(lg[:, j * LANES:(j + 1) * LANES], 0.0) * wb
        for j in range(nslab):
            kpos = off + j * LANES + lane
            sc_ref[:, pl.ds(off + j * LANES, LANES)] = jnp.where(kpos < lim, acc[j], -jnp.inf)
        return carry

    lax.fori_loop(0, nch, idx_body, 0)

    def count(pred):
        def body(c, cnt):
            off = pl.multiple_of(c * kc, kc)
            for j in range(nslab):
                x = sc_ref[:, pl.ds(off + j * LANES, LANES)]
                cnt = cnt + jnp.where(pred(x, off + j * LANES + lane), 1.0, 0.0)
            return cnt
        cnt = lax.fori_loop(0, nch, body, jnp.zeros((tq, LANES), F32))
        return jnp.broadcast_to(jnp.sum(cnt, axis=-1, keepdims=True), (tq, LANES))

    kf = jnp.float32(top_k)

    def bit_body(it, cur):
        cand = cur + lax.shift_left(jnp.int32(1), 31 - it)
        pf = _key_to_float(cand)
        cnt = count(lambda x, _: x >= pf)
        return jnp.where(cnt >= kf, cand, cur)

    cur = lax.fori_loop(0, 32, bit_body, jnp.full((tq, LANES), -2 ** 31, jnp.int32))
    thr = _key_to_float(cur)
    c_gt = count(lambda x, _: x > thr)
    c_ge = count(lambda x, _: x >= thr)
    need = kf - c_gt
    sel_all = lim <= top_k
    excess = jnp.where(jnp.logical_and(c_ge > kf, jnp.logical_not(sel_all)), 1.0, 0.0)

    j_ref[...] = jnp.full((tq, LANES), seq, jnp.int32)

    @pl.when(jnp.max(excess) > 0.0)
    def _():
        def jbit(it, jc):
            cand = jc + lax.shift_left(jnp.int32(1), idx_bits - 1 - it)
            cnt = count(lambda x, kp: jnp.logical_and(x == thr, kp < cand))
            return jnp.where(cnt < need, cand, jc)
        j_ref[...] = lax.fori_loop(0, idx_bits, jbit, jnp.zeros((tq, LANES), jnp.int32))

    jlim = j_ref[...]

    def bias_body(c, carry):
        off = pl.multiple_of(c * kc, kc)
        for j in range(nslab):
            kpos = off + j * LANES + lane
            x = sc_ref[:, pl.ds(off + j * LANES, LANES)]
            picked = jnp.logical_or(x > thr, jnp.logical_and(x == thr, kpos <= jlim))
            keep = jnp.logical_or(jnp.logical_and(sel_all, kpos < lim),
                                  jnp.logical_and(jnp.logical_not(sel_all), picked))
            sc_ref[:, pl.ds(off + j * LANES, LANES)] = jnp.where(keep, 0.0, -jnp.inf)
        return carry

    lax.fori_loop(0, nch, bias_body, 0)

    for h in range(ATT_HEADS):
        qh = q_ref[:, h * HEAD_DIM:(h + 1) * HEAD_DIM]

        def att_body(c, carry, qh=qh):
            m, l, acc = carry
            off = pl.multiple_of(c * kc, kc)
            kvc = kv_ref[pl.ds(off, kc), :]
            sc = lax.dot_general(qh, kvc[:, :HEAD_DIM], (((1,), (1,)), ((), ())),
                                 preferred_element_type=F32)
            sc = sc + sc_ref[:, pl.ds(off, kc)]
            m_new = jnp.maximum(m, jnp.max(sc, axis=-1, keepdims=True))
            p = jnp.exp(sc - m_new)
            alpha = jnp.exp(m - m_new)
            l = alpha * l + jnp.sum(p, axis=-1, keepdims=True)
            acc = alpha * acc + jnp.dot(p.astype(BF16), kvc[:, HEAD_DIM:],
                                        preferred_element_type=F32)
            return m_new, l, acc

        m0 = jnp.full((tq, 1), -1e30, F32)
        m, l, acc = lax.fori_loop(0, nch, att_body,
                                  (m0, jnp.zeros((tq, 1), F32), jnp.zeros((tq, HEAD_DIM), F32)))
        o_ref[:, h * HEAD_DIM:(h + 1) * HEAD_DIM] = (acc / l).astype(BF16)


def _dsa(q, qi, wi, kv, ki, bsz, seq, tq, kc):
    nt = seq // tq
    top_k = min(MAX_TOPK, seq // 4)
    tok = lambda n: pl.BlockSpec((tq, n), lambda b, i: (b * nt + i, 0))
    per_batch = lambda n: pl.BlockSpec((seq, n), lambda b, i: (b, 0))
    return pl.pallas_call(
        functools.partial(_dsa_kernel, tq=tq, kc=kc, seq=seq, top_k=top_k),
        grid=(bsz, nt),
        in_specs=[tok(512), tok(512), tok(128), per_batch(128), per_batch(128)],
        out_specs=tok(512),
        out_shape=jax.ShapeDtypeStruct((bsz * seq, ATT_HEADS * HEAD_DIM), BF16),
        scratch_shapes=[pltpu.VMEM((tq, seq), F32),
                        pltpu.VMEM((IDX_HEADS, tq, LANES), F32),
                        pltpu.VMEM((tq, LANES), jnp.int32)],
        compiler_params=_params(("parallel", "arbitrary")),
    )(q, qi, wi, kv, ki)


def _merge_kernel(x_ref, ya_ref, yb_ref, yc_ref, gl_ref, bg_ref, wb_ref, wo_ref, g_ref, o_ref):
    merged = None
    for n, y_ref in enumerate((ya_ref, yb_ref, yc_ref)):
        cs = slice(n * D_MODEL, (n + 1) * D_MODEL)
        up = jnp.dot(y_ref[...], wb_ref[n], preferred_element_type=F32)
        term = jax.nn.sigmoid(gl_ref[:, cs] + bg_ref[:, cs]) * up
        merged = term if merged is None else merged + term
    out = jnp.dot(merged.astype(BF16), wo_ref[...], preferred_element_type=F32)
    o_ref[...] = x_ref[...] + _rms(out, g_ref[...])


def _merge(xt, ya, yb, yc, gl, bg, wb, wo, g, tm):
    t = xt.shape[0]
    row = lambda n: pl.BlockSpec((tm, n), lambda i: (i, 0))
    return pl.pallas_call(
        _merge_kernel,
        grid=(t // tm,),
        in_specs=[row(D_MODEL), row(512), row(512), row(512), row(N_BRANCH * D_MODEL),
                  _resident((1, N_BRANCH * D_MODEL)),
                  _resident((N_BRANCH, BRANCH_WIDTH, D_MODEL)),
                  _resident((D_MODEL, D_MODEL)), _resident((1, D_MODEL))],
        out_specs=row(D_MODEL),
        out_shape=jax.ShapeDtypeStruct((t, D_MODEL), F32),
        compiler_params=_params(("parallel",)),
    )(xt, ya, yb, yc, gl, bg, wb, wo, g)


def _mem_kv_kernel(m_ref, g_ref, w_ref, k_ref, v_ref):
    mn = _rms(m_ref[...], g_ref[...]).astype(BF16)
    k_ref[...] = jnp.dot(mn, w_ref[:, :MEM_WIDTH], preferred_element_type=F32).astype(BF16)
    v_ref[...] = jnp.dot(mn, w_ref[:, MEM_WIDTH:], preferred_element_type=F32).astype(BF16)


def _mem_kv(memt, g, w, bsz):
    blk = lambda n: pl.BlockSpec((N_MEM, n), lambda b: (b, 0))
    return pl.pallas_call(
        _mem_kv_kernel,
        grid=(bsz,),
        in_specs=[blk(D_MODEL), _resident((1, D_MODEL)), _resident((D_MODEL, 2 * MEM_WIDTH))],
        out_specs=[blk(MEM_WIDTH), blk(MEM_WIDTH)],
        out_shape=[jax.ShapeDtypeStruct((bsz * N_MEM, MEM_WIDTH), BF16)] * 2,
        compiler_params=_params(("parallel",)),
    )(memt, g, w)


def _mem_attn_kernel(x_ref, k_ref, v_ref, gp_ref, wq_ref, wo_ref, go_ref, o_ref):
    x = x_ref[...]
    h = _rms(x, gp_ref[...]).astype(BF16)
    q = jnp.dot(h, wq_ref[...], preferred_element_type=F32).astype(BF16)
    heads = []
    for hd in range(MEM_HEADS):
        cs = slice(hd * MEM_HEAD_DIM, (hd + 1) * MEM_HEAD_DIM)
        att = lax.dot_general(q[:, cs], k_ref[:, cs], (((1,), (1,)), ((), ())),
                              preferred_element_type=F32) * (MEM_HEAD_DIM ** -0.5)
        p = jnp.exp(att - jnp.max(att, axis=-1, keepdims=True))
        prob = (p / jnp.sum(p, axis=-1, keepdims=True)).astype(BF16)
        heads.append(jnp.dot(prob, v_ref[:, cs], preferred_element_type=F32).astype(BF16))
    o = jnp.concatenate(heads, axis=-1)
    out = jnp.dot(o, wo_ref[...], preferred_element_type=F32)
    o_ref[...] = x + _rms(out, go_ref[...])


def _mem_attn(xt, kmem, vmem, gp, wq, wo, go, bsz, seq, tm):
    nt = seq // tm
    row = pl.BlockSpec((tm, D_MODEL), lambda b, i: (b * nt + i, 0))
    memblk = pl.BlockSpec((N_MEM, MEM_WIDTH), lambda b, i: (b, 0))
    return pl.pallas_call(
        _mem_attn_kernel,
        grid=(bsz, nt),
        in_specs=[row, memblk, memblk, _resident((1, D_MODEL)),
                  _resident((D_MODEL, MEM_WIDTH)), _resident((MEM_WIDTH, D_MODEL)),
                  _resident((1, D_MODEL))],
        out_specs=row,
        out_shape=jax.ShapeDtypeStruct(xt.shape, F32),
        compiler_params=_params(("parallel", "parallel")),
    )(xt, kmem, vmem, gp, wq, wo, go)


FF_CHUNK = 256


def _ffn_kernel(x_ref, gp_ref, wi_ref, wo_ref, go_ref, o_ref, acc_ref):
    x = x_ref[...]
    h = _rms(x, gp_ref[...]).astype(BF16)
    for c in range(D_FF // FF_CHUNK):
        lo = c * FF_CHUNK
        fg = jnp.dot(h, wi_ref[:, lo:lo + FF_CHUNK], preferred_element_type=F32)
        fu = jnp.dot(h, wi_ref[:, D_FF + lo:D_FF + lo + FF_CHUNK], preferred_element_type=F32)
        act = (fg * jax.nn.sigmoid(fg) * fu).astype(BF16)
        part = jnp.dot(act, wo_ref[lo:lo + FF_CHUNK, :], preferred_element_type=F32)
        if c == 0:
            acc_ref[...] = part
        else:
            acc_ref[...] += part
    o_ref[...] = x + _rms(acc_ref[...], go_ref[...])


def _ffn(xt, gp, wi, wo, go, tm):
    t = xt.shape[0]
    row = pl.BlockSpec((tm, D_MODEL), lambda i: (i, 0))
    return pl.pallas_call(
        _ffn_kernel,
        grid=(t // tm,),
        in_specs=[row, _resident((1, D_MODEL)), _resident((D_MODEL, 2 * D_FF)),
                  _resident((D_FF, D_MODEL)), _resident((1, D_MODEL))],
        out_specs=row,
        out_shape=jax.ShapeDtypeStruct(xt.shape, F32),
        scratch_shapes=[pltpu.VMEM((tm, D_MODEL), F32)],
        compiler_params=_params(("parallel",)),
    )(xt, gp, wi, wo, go)


def _pack_w_in(w):
    z = lambda n: jnp.zeros((D_MODEL, n), w.dtype)
    o_k = 4 * BRANCH_WIDTH
    o_qi = o_k + 2 * HEAD_DIM
    o_ki = o_qi + IDX_HEADS * IDX_DIM
    o_wi = o_ki + IDX_DIM
    o_gl = o_wi + IDX_HEADS
    packed = jnp.concatenate([
        w[:, :o_ki],
        w[:, o_ki:o_wi], z(LANES - IDX_DIM),
        w[:, o_wi:o_gl], z(LANES - IDX_HEADS),
        w[:, o_gl:],
    ], axis=1)
    return packed.astype(BF16)


def _block_diag(w):
    nb, n, _ = w.shape
    eye = jnp.eye(nb, dtype=w.dtype)
    return (eye[:, None, :, None] * w[:, :, None, :]).reshape(nb * n, nb * n)


def kernel(x, mem, g_mix_pre, w_in, conv_w, conv_b, lru_w_a, lru_b_a, lru_w_x, lru_b_x,
           lru_a_param, w_pool, pool_scale, w_branch, b_gate, w_out, g_mix_post,
           g_mem_pre, g_mem_kv, w_mem_q, w_mem_kv, w_mem_o, g_mem_post,
           g_ffn_pre, w_ffn_in, w_ffn_out, g_ffn_post):
    bsz, seq, d = x.shape
    assert d == D_MODEL and seq % 512 == 0 and (seq & (seq - 1)) == 0
    t = bsz * seq
    tm = 256 if t % 256 == 0 else t
    ts = min(256, seq)
    tq = 128
    kc = 512

    vec = lambda a: a.reshape(a.shape[0], 1, -1)
    layers = dict(
        g_mix_pre=vec(g_mix_pre), w_in=w_in, conv_w=conv_w, conv_b=vec(conv_b),
        lru_w_a=lru_w_a, lru_b_a=vec(lru_b_a), lru_w_x=lru_w_x, lru_b_x=vec(lru_b_x),
        lru_a_param=vec(lru_a_param), w_pool=w_pool, pool_scale=vec(pool_scale),
        w_branch=w_branch, b_gate=b_gate.reshape(b_gate.shape[0], 1, -1), w_out=w_out,
        g_mix_post=vec(g_mix_post), g_mem_pre=vec(g_mem_pre), g_mem_kv=vec(g_mem_kv),
        w_mem_q=w_mem_q, w_mem_kv=w_mem_kv, w_mem_o=w_mem_o, g_mem_post=vec(g_mem_post),
        g_ffn_pre=vec(g_ffn_pre), w_ffn_in=w_ffn_in, w_ffn_out=w_ffn_out,
        g_ffn_post=vec(g_ffn_post))
    memt = mem.reshape(bsz * N_MEM, D_MODEL)

    def layer(xt, p):
        lx, lg, pin, q, kv, qi, ki, wi, gl = _mix_in(xt, p["g_mix_pre"], _pack_w_in(p["w_in"]), tm)
        ya, yb = _lru_pool(lx, lg, pin, p["conv_w"], p["conv_b"],
                           _block_diag(p["lru_w_a"]).astype(BF16), p["lru_b_a"],
                           _block_diag(p["lru_w_x"]).astype(BF16), p["lru_b_x"],
                           p["lru_a_param"], p["w_pool"].astype(BF16), p["pool_scale"],
                           bsz, seq, ts)
        yc = _dsa(q, qi, wi, kv, ki, bsz, seq, tq, kc)
        xt = _merge(xt, ya, yb, yc, gl, p["b_gate"], p["w_branch"].astype(BF16),
                    p["w_out"].astype(BF16), p["g_mix_post"], tm)
        kmem, vmem = _mem_kv(memt, p["g_mem_kv"], p["w_mem_kv"].astype(BF16), bsz)
        xt = _mem_attn(xt, kmem, vmem, p["g_mem_pre"], p["w_mem_q"].astype(BF16),
                       p["w_mem_o"].astype(BF16), p["g_mem_post"], bsz, seq, min(512, seq))
        xt = _ffn(xt, p["g_ffn_pre"], p["w_ffn_in"].astype(BF16),
                  p["w_ffn_out"].astype(BF16), p["g_ffn_post"], min(512, t))
        return xt, None

    xt, _ = lax.scan(layer, x.reshape(t, D_MODEL), layers)
    return xt.reshape(bsz, seq, D_MODEL)
```

```python
import functools

import jax
import jax.numpy as jnp
from jax import lax
from jax.experimental import pallas as pl
from jax.experimental.pallas import tpu as pltpu

F32 = jnp.float32
BF16 = jnp.bfloat16

D_MODEL = 1024
CHUNK = 64
CHUNK_SHIFT = 6
N_MEM = 256
BRANCH_WIDTH = 512
N_BRANCH = 3
LRU_BLOCKS = 8
CONV_WIDTH = 4
LRU_C = 8.0
POOL_WINDOWS = (2, 4, 8, 16)
POOL_GROUP = BRANCH_WIDTH // len(POOL_WINDOWS)
ATT_HEADS = 8
HEAD_DIM = 64
IDX_HEADS = 8
IDX_DIM = 64
MAX_TOPK = 256
MEM_HEADS = 4
MEM_HEAD_DIM = 128
MEM_WIDTH = MEM_HEADS * MEM_HEAD_DIM
D_FF = 2816
RMS_EPS = 1e-6

LANES = 128
VMEM_LIMIT = 56 * 1024 * 1024

C_LX, C_LG, C_PIN, C_Q = 0, 512, 1024, 1536
C_KV = 2048
C_QI = 2176
C_KI = 2688
C_WI = 2816
C_GL = 2944
D_PACK = C_GL + N_BRANCH * D_MODEL


def _params(sem, vmem=VMEM_LIMIT):
    return pltpu.CompilerParams(dimension_semantics=sem, vmem_limit_bytes=vmem)


def _resident(shape):
    nd = len(shape)
    return pl.BlockSpec(shape, lambda *_: (0,) * nd, pipeline_mode=pl.Buffered(1))


def _rms(x, g):
    return x * lax.rsqrt(jnp.mean(x * x, axis=-1, keepdims=True) + RMS_EPS) * g


def _mix_in_kernel(x_ref, g_ref, w_ref, lx_ref, lg_ref, pin_ref, q_ref, kv_ref,
                   qi_ref, ki_ref, wi_ref, gl_ref):
    h = _rms(x_ref[...], g_ref[...]).astype(BF16)

    def proj(lo, hi):
        return jnp.dot(h, w_ref[:, lo:hi], preferred_element_type=F32)

    lx_ref[...] = proj(C_LX, C_LG)
    lg_ref[...] = proj(C_LG, C_PIN)
    pin_ref[...] = proj(C_PIN, C_Q)
    q_ref[...] = (proj(C_Q, C_KV) * (HEAD_DIM ** -0.5)).astype(BF16)
    kv_ref[...] = proj(C_KV, C_QI).astype(BF16)
    qi_ref[...] = (proj(C_QI, C_KI) * (IDX_DIM ** -0.5)).astype(BF16)
    ki_ref[...] = proj(C_KI, C_WI).astype(BF16)
    wi_ref[...] = proj(C_WI, C_GL) * (IDX_HEADS ** -0.5)
    gl_ref[...] = proj(C_GL, D_PACK)


def _mix_in(xt, g, w, tm):
    t = xt.shape[0]
    row = lambda n: pl.BlockSpec((tm, n), lambda i: (i, 0))
    outs = [(512, F32), (512, F32), (512, F32), (512, BF16), (128, BF16),
            (512, BF16), (128, BF16), (128, F32), (N_BRANCH * D_MODEL, F32)]
    return pl.pallas_call(
        _mix_in_kernel,
        grid=(t // tm,),
        in_specs=[row(D_MODEL), _resident((1, D_MODEL)), _resident((D_MODEL, D_PACK))],
        out_specs=[row(n) for n, _ in outs],
        out_shape=[jax.ShapeDtypeStruct((t, n), dt) for n, dt in outs],
        compiler_params=_params(("parallel",)),
    )(xt, g, w)


CONV_PAD = 8
POOL_PAD = 16


def _log1p(x):
    u = 1.0 + x
    return jnp.where(u == 1.0, x, jnp.log(u) * (x / (u - 1.0)))


def _softplus(x):
    return jnp.maximum(x, 0.0) + _log1p(jnp.exp(-jnp.abs(x)))


def _lru_pool_kernel(lx_ref, lg_ref, pin_ref, cw_ref, cb_ref, wa_ref, ba_ref, wx_ref,
                     bx_ref, ap_ref, wp_ref, ps_ref, ya_ref, yb_ref,
                     xbuf, pbuf, hc_ref, *, ts):
    s = pl.program_id(1)

    @pl.when(s == 0)
    def _():
        xbuf[0:CONV_PAD, :] = jnp.zeros((CONV_PAD, BRANCH_WIDTH), F32)
        pbuf[0:POOL_PAD, :] = jnp.zeros((POOL_PAD, BRANCH_WIDTH), F32)
        hc_ref[...] = jnp.zeros_like(hc_ref)

    xbuf[CONV_PAD:CONV_PAD + ts, :] = lx_ref[...]
    ua = cb_ref[...] + cw_ref[CONV_WIDTH - 1:CONV_WIDTH, :] * xbuf[CONV_PAD:CONV_PAD + ts, :]
    for j in range(CONV_WIDTH - 1):
        off = CONV_PAD - (CONV_WIDTH - 1) + j
        ua = ua + cw_ref[j:j + 1, :] * xbuf[off:off + ts, :]
    xbuf[0:CONV_PAD, :] = xbuf[ts:ts + CONV_PAD, :]

    ub = ua.astype(BF16)
    r = jax.nn.sigmoid(jnp.dot(ub, wa_ref[...], preferred_element_type=F32) + ba_ref[...])
    ig = jax.nn.sigmoid(jnp.dot(ub, wx_ref[...], preferred_element_type=F32) + bx_ref[...])
    log_a = -LRU_C * r * _softplus(-ap_ref[...])
    a = jnp.exp(log_a)
    mult = jnp.sqrt(jnp.maximum(1.0 - jnp.exp(2.0 * log_a), 0.0))
    u = ua * ig * mult

    row = lax.broadcasted_iota(jnp.int32, (ts, LANES), 0)
    for c in range(BRANCH_WIDTH // LANES):
        cs = slice(c * LANES, (c + 1) * LANES)
        ac, hcur = a[:, cs], u[:, cs]
        sh = 1
        while sh < ts:
            keep = row >= sh
            a_sh = jnp.where(keep, pltpu.roll(ac, sh, 0), 1.0)
            h_sh = jnp.where(keep, pltpu.roll(hcur, sh, 0), 0.0)
            hcur = ac * h_sh + hcur
            ac = ac * a_sh
            sh *= 2
        hfull = hcur + ac * hc_ref[:, cs]
        hc_ref[:, cs] = hfull[ts - 1:ts, :]
        ya_ref[:, cs] = (hfull * jax.nn.gelu(lg_ref[:, cs], approximate=True)).astype(BF16)

    pbuf[POOL_PAD:POOL_PAD + ts, :] = pin_ref[...]
    tpos = s * ts + row
    for gi, win in enumerate(POOL_WINDOWS):
        cs = slice(gi * POOL_GROUP, (gi + 1) * POOL_GROUP)
        cur = pbuf[POOL_PAD:POOL_PAD + ts, cs]
        acc = cur
        for j in range(1, win):
            acc = acc + pbuf[POOL_PAD - j:POOL_PAD - j + ts, cs]
        count = jnp.minimum(tpos + 1, win).astype(F32)
        pooled = (acc / count - cur).astype(BF16)
        mixed = jnp.dot(pooled, wp_ref[gi], preferred_element_type=F32)
        yb_ref[:, cs] = (mixed * ps_ref[:, cs]).astype(BF16)
    pbuf[0:POOL_PAD, :] = pbuf[ts:ts + POOL_PAD, :]


def _lru_pool(lx, lg, pin, cw, cb, wa, ba, wx, bx, ap, wp, ps, bsz, seq, ts):
    nt = seq // ts
    tok = pl.BlockSpec((ts, BRANCH_WIDTH), lambda b, s: (b * nt + s, 0))
    vec = _resident((1, BRANCH_WIDTH))
    return pl.pallas_call(
        functools.partial(_lru_pool_kernel, ts=ts),
        grid=(bsz, nt),
        in_specs=[tok, tok, tok, _resident((CONV_WIDTH, BRANCH_WIDTH)), vec,
                  _resident((BRANCH_WIDTH, BRANCH_WIDTH)), vec,
                  _resident((BRANCH_WIDTH, BRANCH_WIDTH)), vec, vec,
                  _resident((len(POOL_WINDOWS), POOL_GROUP, POOL_GROUP)), vec],
        out_specs=[tok, tok],
        out_shape=[jax.ShapeDtypeStruct((bsz * seq, BRANCH_WIDTH), BF16)] * 2,
        scratch_shapes=[pltpu.VMEM((CONV_PAD + ts, BRANCH_WIDTH), F32),
                        pltpu.VMEM((POOL_PAD + ts, BRANCH_WIDTH), F32),
                        pltpu.VMEM((1, BRANCH_WIDTH), F32)],
        compiler_params=_params(("arbitrary", "arbitrary")),
    )(lx, lg, pin, cw, cb, wa, ba, wx, bx, ap, wp, ps)


def _key_to_float(key):
    bits = key ^ (lax.shift_right_arithmetic(key, 31) & jnp.int32(0x7FFFFFFF))
    return lax.bitcast_convert_type(bits, F32)


def _dsa_kernel(qt_ref, qit_ref, wt_ref, kv_ref, ki_ref, vt_ref, o_ref, sc_ref,
                *, tq, kc, seq, top_k):
    i = pl.program_id(1)
    nch = ((i + 1) * tq + kc - 1) // kc
    idx_bits = seq.bit_length() - 1

    lane = lax.broadcasted_iota(jnp.int32, (1, tq), 1)
    krow = lax.broadcasted_iota(jnp.int32, (kc, tq), 0)
    qpos = i * tq + lane
    lim = (lax.shift_right_logical(qpos, CHUNK_SHIFT) + 1) * CHUNK

    def idx_body(c, carry):
        off = pl.multiple_of(c * kc, kc)
        lt = jnp.dot(ki_ref[pl.ds(off, kc), :IDX_DIM], qit_ref[...],
                     preferred_element_type=F32)
        acc = None
        for h in range(IDX_HEADS):
            term = jnp.maximum(lt[:, h * tq:(h + 1) * tq], 0.0) * wt_ref[h:h + 1, :]
            acc = term if acc is None else acc + term
        sc_ref[pl.ds(off, kc), :] = jnp.where(off + krow < lim, acc, -jnp.inf)
        return carry

    lax.fori_loop(0, nch, idx_body, 0)

    def count(pred):
        def body(c, cnt):
            off = pl.multiple_of(c * kc, kc)
            hit = jnp.where(pred(sc_ref[pl.ds(off, kc), :], off + krow), 1.0, 0.0)
            return cnt + jnp.sum(hit.reshape(kc // 8, 8, tq), axis=0)
        cnt = lax.fori_loop(0, nch, body, jnp.zeros((8, tq), F32))
        return jnp.sum(cnt, axis=0, keepdims=True)

    kf = jnp.float32(top_k)

    def bit_body(it, cur):
        cand = cur + lax.shift_left(jnp.int32(1), 31 - it)
        pf = _key_to_float(cand)
        cnt = count(lambda x, _: x >= pf)
        return jnp.where(cnt >= kf, cand, cur)

    cur = lax.fori_loop(0, 32, bit_body, jnp.full((1, tq), -2 ** 31, jnp.int32))
    thr = _key_to_float(cur)
    c_gt = count(lambda x, _: x > thr)
    c_ge = count(lambda x, _: x >= thr)
    need = kf - c_gt
    sel_all = lim <= top_k
    excess = jnp.where(jnp.logical_and(c_ge > kf, jnp.logical_not(sel_all)), 1.0, 0.0)

    def tie_limit():
        def jbit(it, jc):
            cand = jc + lax.shift_left(jnp.int32(1), idx_bits - 1 - it)
            cnt = count(lambda x, kp: jnp.logical_and(x == thr, kp < cand))
            return jnp.where(cnt < need, cand, jc)
        return lax.fori_loop(0, idx_bits, jbit, jnp.zeros((1, tq), jnp.int32))

    jlim = lax.cond(jnp.max(excess) > 0.0, tie_limit,
                    lambda: jnp.full((1, tq), seq, jnp.int32))

    def bias_body(c, carry):
        off = pl.multiple_of(c * kc, kc)
        kpos = off + krow
        x = sc_ref[pl.ds(off, kc), :]
        picked = jnp.logical_or(x > thr, jnp.logical_and(x == thr, kpos <= jlim))
        keep = jnp.logical_or(jnp.logical_and(sel_all, kpos < lim),
                              jnp.logical_and(jnp.logical_not(sel_all), picked))
        sc_ref[pl.ds(off, kc), :] = jnp.where(keep, 0.0, -jnp.inf)
        return carry

    lax.fori_loop(0, nch, bias_body, 0)

    nq = ATT_HEADS * tq

    def att_body(c, carry):
        m, l, acc = carry
        off = pl.multiple_of(c * kc, kc)
        st = jnp.dot(kv_ref[pl.ds(off, kc), :HEAD_DIM], qt_ref[...],
                     preferred_element_type=F32)
        bias = sc_ref[pl.ds(off, kc), :]
        sh = [st[:, h * tq:(h + 1) * tq] + bias for h in range(ATT_HEADS)]
        cmax = jnp.concatenate([jnp.max(s, axis=0, keepdims=True) for s in sh], axis=1)
        m_new = jnp.maximum(m, cmax)
        alpha = jnp.exp(m - m_new)
        ps = [jnp.exp(sh[h] - m_new[:, h * tq:(h + 1) * tq]) for h in range(ATT_HEADS)]
        l = alpha * l + jnp.concatenate([jnp.sum(p, axis=0, keepdims=True) for p in ps], axis=1)
        pt = jnp.concatenate([p.astype(BF16) for p in ps], axis=1)
        acc = alpha * acc + jnp.dot(vt_ref[:, pl.ds(off, kc)], pt, preferred_element_type=F32)
        return m_new, l, acc

    m0 = jnp.full((1, nq), -1e30, F32)
    m, l, acc = lax.fori_loop(0, nch, att_body,
                              (m0, jnp.zeros((1, nq), F32), jnp.zeros((HEAD_DIM, nq), F32)))
    o_ref[...] = (acc / l).astype(BF16)


def _dsa(q, qi, wi, kv, ki, bsz, seq, tq, kc):
    nt = seq // tq
    nq = ATT_HEADS * tq
    top_k = min(MAX_TOPK, seq // 4)

    def heads_to_lanes(a):
        a = a.reshape(bsz, nt, tq, ATT_HEADS, HEAD_DIM).transpose(0, 1, 4, 3, 2)
        return a.reshape(bsz * nt * HEAD_DIM, nq)

    qt, qit = heads_to_lanes(q), heads_to_lanes(qi)
    wt = wi[:, :IDX_HEADS].reshape(bsz, nt, tq, IDX_HEADS).transpose(0, 1, 3, 2)
    wt = wt.reshape(bsz * nt * IDX_HEADS, tq)
    vt = kv[:, HEAD_DIM:].reshape(bsz, seq, HEAD_DIM).transpose(0, 2, 1).reshape(bsz * HEAD_DIM, seq)

    tile = lambda r, n: pl.BlockSpec((r, n), lambda b, i: (b * nt + i, 0))
    per_batch = lambda r, n: pl.BlockSpec((r, n), lambda b, i: (b, 0))
    ot = pl.pallas_call(
        functools.partial(_dsa_kernel, tq=tq, kc=kc, seq=seq, top_k=top_k),
        grid=(bsz, nt),
        in_specs=[tile(HEAD_DIM, nq), tile(IDX_DIM, nq), tile(IDX_HEADS, tq),
                  per_batch(seq, 2 * HEAD_DIM), per_batch(seq, LANES), per_batch(HEAD_DIM, seq)],
        out_specs=tile(HEAD_DIM, nq),
        out_shape=jax.ShapeDtypeStruct((bsz * nt * HEAD_DIM, nq), BF16),
        scratch_shapes=[pltpu.VMEM((seq, tq), F32)],
        compiler_params=_params(("parallel", "arbitrary")),
    )(qt, qit, wt, kv, ki, vt)
    ot = ot.reshape(bsz, nt, HEAD_DIM, ATT_HEADS, tq).transpose(0, 1, 4, 3, 2)
    return ot.reshape(bsz * seq, ATT_HEADS * HEAD_DIM)


def _merge_kernel(x_ref, ya_ref, yb_ref, yc_ref, gl_ref, bg_ref, wb_ref, wo_ref, g_ref, o_ref):
    merged = None
    for n, y_ref in enumerate((ya_ref, yb_ref, yc_ref)):
        cs = slice(n * D_MODEL, (n + 1) * D_MODEL)
        up = jnp.dot(y_ref[...], wb_ref[n], preferred_element_type=F32)
        term = jax.nn.sigmoid(gl_ref[:, cs] + bg_ref[:, cs]) * up
        merged = term if merged is None else merged + term
    out = jnp.dot(merged.astype(BF16), wo_ref[...], preferred_element_type=F32)
    o_ref[...] = x_ref[...] + _rms(out, g_ref[...])


def _merge(xt, ya, yb, yc, gl, bg, wb, wo, g, tm):
    t = xt.shape[0]
    row = lambda n: pl.BlockSpec((tm, n), lambda i: (i, 0))
    return pl.pallas_call(
        _merge_kernel,
        grid=(t // tm,),
        in_specs=[row(D_MODEL), row(512), row(512), row(512), row(N_BRANCH * D_MODEL),
                  _resident((1, N_BRANCH * D_MODEL)),
                  _resident((N_BRANCH, BRANCH_WIDTH, D_MODEL)),
                  _resident((D_MODEL, D_MODEL)), _resident((1, D_MODEL))],
        out_specs=row(D_MODEL),
        out_shape=jax.ShapeDtypeStruct((t, D_MODEL), F32),
        compiler_params=_params(("parallel",)),
    )(xt, ya, yb, yc, gl, bg, wb, wo, g)


def _mem_kv_kernel(m_ref, g_ref, w_ref, k_ref, v_ref):
    mn = _rms(m_ref[...], g_ref[...]).astype(BF16)
    k_ref[...] = jnp.dot(mn, w_ref[:, :MEM_WIDTH], preferred_element_type=F32).astype(BF16)
    v_ref[...] = jnp.dot(mn, w_ref[:, MEM_WIDTH:], preferred_element_type=F32).astype(BF16)


def _mem_kv(memt, g, w, bsz):
    blk = lambda n: pl.BlockSpec((N_MEM, n), lambda b: (b, 0))
    return pl.pallas_call(
        _mem_kv_kernel,
        grid=(bsz,),
        in_specs=[blk(D_MODEL), _resident((1, D_MODEL)), _resident((D_MODEL, 2 * MEM_WIDTH))],
        out_specs=[blk(MEM_WIDTH), blk(MEM_WIDTH)],
        out_shape=[jax.ShapeDtypeStruct((bsz * N_MEM, MEM_WIDTH), BF16)] * 2,
        compiler_params=_params(("parallel",)),
    )(memt, g, w)


def _mem_attn_kernel(x_ref, k_ref, v_ref, gp_ref, wq_ref, wo_ref, go_ref, o_ref):
    x = x_ref[...]
    h = _rms(x, gp_ref[...]).astype(BF16)
    q = jnp.dot(h, wq_ref[...], preferred_element_type=F32).astype(BF16)
    heads = []
    for hd in range(MEM_HEADS):
        cs = slice(hd * MEM_HEAD_DIM, (hd + 1) * MEM_HEAD_DIM)
        att = lax.dot_general(q[:, cs], k_ref[:, cs], (((1,), (1,)), ((), ())),
                              preferred_element_type=F32) * (MEM_HEAD_DIM ** -0.5)
        p = jnp.exp(att - jnp.max(att, axis=-1, keepdims=True))
        prob = (p / jnp.sum(p, axis=-1, keepdims=True)).astype(BF16)
        heads.append(jnp.dot(prob, v_ref[:, cs], preferred_element_type=F32).astype(BF16))
    o = jnp.concatenate(heads, axis=-1)
    out = jnp.dot(o, wo_ref[...], preferred_element_type=F32)
    o_ref[...] = x + _rms(out, go_ref[...])


def _mem_attn(xt, kmem, vmem, gp, wq, wo, go, bsz, seq, tm):
    nt = seq // tm
    row = pl.BlockSpec((tm, D_MODEL), lambda b, i: (b * nt + i, 0))
    memblk = pl.BlockSpec((N_MEM, MEM_WIDTH), lambda b, i: (b, 0))
    return pl.pallas_call(
        _mem_attn_kernel,
        grid=(bsz, nt),
        in_specs=[row, memblk, memblk, _resident((1, D_MODEL)),
                  _resident((D_MODEL, MEM_WIDTH)), _resident((MEM_WIDTH, D_MODEL)),
                  _resident((1, D_MODEL))],
        out_specs=row,
        out_shape=jax.ShapeDtypeStruct(xt.shape, F32),
        compiler_params=_params(("parallel", "parallel")),
    )(xt, kmem, vmem, gp, wq, wo, go)


FF_CHUNK = 256


def _ffn_kernel(x_ref, gp_ref, wi_ref, wo_ref, go_ref, o_ref, acc_ref):
    x = x_ref[...]
    h = _rms(x, gp_ref[...]).astype(BF16)
    for c in range(D_FF // FF_CHUNK):
        lo = c * FF_CHUNK
        fg = jnp.dot(h, wi_ref[:, lo:lo + FF_CHUNK], preferred_element_type=F32)
        fu = jnp.dot(h, wi_ref[:, D_FF + lo:D_FF + lo + FF_CHUNK], preferred_element_type=F32)
        act = (fg * jax.nn.sigmoid(fg) * fu).astype(BF16)
        part = jnp.dot(act, wo_ref[lo:lo + FF_CHUNK, :], preferred_element_type=F32)
        if c == 0:
            acc_ref[...] = part
        else:
            acc_ref[...] += part
    o_ref[...] = x + _rms(acc_ref[...], go_ref[...])


def _ffn(xt, gp, wi, wo, go, tm):
    t = xt.shape[0]
    row = pl.BlockSpec((tm, D_MODEL), lambda i: (i, 0))
    return pl.pallas_call(
        _ffn_kernel,
        grid=(t // tm,),
        in_specs=[row, _resident((1, D_MODEL)), _resident((D_MODEL, 2 * D_FF)),
                  _resident((D_FF, D_MODEL)), _resident((1, D_MODEL))],
        out_specs=row,
        out_shape=jax.ShapeDtypeStruct(xt.shape, F32),
        scratch_shapes=[pltpu.VMEM((tm, D_MODEL), F32)],
        compiler_params=_params(("parallel",)),
    )(xt, gp, wi, wo, go)


def _pack_w_in(w):
    z = lambda n: jnp.zeros((D_MODEL, n), w.dtype)
    o_k = 4 * BRANCH_WIDTH
    o_qi = o_k + 2 * HEAD_DIM
    o_ki = o_qi + IDX_HEADS * IDX_DIM
    o_wi = o_ki + IDX_DIM
    o_gl = o_wi + IDX_HEADS
    packed = jnp.concatenate([
        w[:, :o_ki],
        w[:, o_ki:o_wi], z(LANES - IDX_DIM),
        w[:, o_wi:o_gl], z(LANES - IDX_HEADS),
        w[:, o_gl:],
    ], axis=1)
    return packed.astype(BF16)


def _block_diag(w):
    nb, n, _ = w.shape
    eye = jnp.eye(nb, dtype=w.dtype)
    return (eye[:, None, :, None] * w[:, :, None, :]).reshape(nb * n, nb * n)


def kernel(x, mem, g_mix_pre, w_in, conv_w, conv_b, lru_w_a, lru_b_a, lru_w_x, lru_b_x,
           lru_a_param, w_pool, pool_scale, w_branch, b_gate, w_out, g_mix_post,
           g_mem_pre, g_mem_kv, w_mem_q, w_mem_kv, w_mem_o, g_mem_post,
           g_ffn_pre, w_ffn_in, w_ffn_out, g_ffn_post):
    bsz, seq, d = x.shape
    assert d == D_MODEL and seq % 512 == 0 and (seq & (seq - 1)) == 0
    t = bsz * seq
    tm = 256 if t % 256 == 0 else t
    ts = min(256, seq)
    tq = min(256, seq)
    kc = tq

    vec = lambda a: a.reshape(a.shape[0], 1, -1)
    layers = dict(
        g_mix_pre=vec(g_mix_pre), w_in=w_in, conv_w=conv_w, conv_b=vec(conv_b),
        lru_w_a=lru_w_a, lru_b_a=vec(lru_b_a), lru_w_x=lru_w_x, lru_b_x=vec(lru_b_x),
        lru_a_param=vec(lru_a_param), w_pool=w_pool, pool_scale=vec(pool_scale),
        w_branch=w_branch, b_gate=b_gate.reshape(b_gate.shape[0], 1, -1), w_out=w_out,
        g_mix_post=vec(g_mix_post), g_mem_pre=vec(g_mem_pre), g_mem_kv=vec(g_mem_kv),
        w_mem_q=w_mem_q, w_mem_kv=w_mem_kv, w_mem_o=w_mem_o, g_mem_post=vec(g_mem_post),
        g_ffn_pre=vec(g_ffn_pre), w_ffn_in=w_ffn_in, w_ffn_out=w_ffn_out,
        g_ffn_post=vec(g_ffn_post))
    memt = mem.reshape(bsz * N_MEM, D_MODEL)

    def layer(xt, p):
        lx, lg, pin, q, kv, qi, ki, wi, gl = _mix_in(xt, p["g_mix_pre"], _pack_w_in(p["w_in"]), tm)
        ya, yb = _lru_pool(lx, lg, pin, p["conv_w"], p["conv_b"],
                           _block_diag(p["lru_w_a"]).astype(BF16), p["lru_b_a"],
                           _block_diag(p["lru_w_x"]).astype(BF16), p["lru_b_x"],
                           p["lru_a_param"], p["w_pool"].astype(BF16), p["pool_scale"],
                           bsz, seq, ts)
        yc = _dsa(q, qi, wi, kv, ki, bsz, seq, tq, kc)
        xt = _merge(xt, ya, yb, yc, gl, p["b_gate"], p["w_branch"].astype(BF16),
                    p["w_out"].astype(BF16), p["g_mix_post"], tm)
        kmem, vmem = _mem_kv(memt, p["g_mem_kv"], p["w_mem_kv"].astype(BF16), bsz)
        xt = _mem_attn(xt, kmem, vmem, p["g_mem_pre"], p["w_mem_q"].astype(BF16),
                       p["w_mem_o"].astype(BF16), p["g_mem_post"], bsz, seq, min(512, seq))
        xt = _ffn(xt, p["g_ffn_pre"], p["w_ffn_in"].astype(BF16),
                  p["w_ffn_out"].astype(BF16), p["g_ffn_post"], min(512, t))
        return xt, None

    xt, _ = lax.scan(layer, x.reshape(t, D_MODEL), layers)
    return xt.reshape(bsz, seq, D_MODEL)
```

```python
import functools
import math

import jax
import jax.numpy as jnp
from jax import lax
from jax.experimental import pallas as pl
from jax.experimental.pallas import tpu as pltpu

F32 = jnp.float32
BF16 = jnp.bfloat16

D_MODEL = 1024
CHUNK = 64
CHUNK_SHIFT = 6
N_MEM = 256
BRANCH_WIDTH = 512
N_BRANCH = 3
LRU_BLOCKS = 8
CONV_WIDTH = 4
LRU_C = 8.0
POOL_WINDOWS = (2, 4, 8, 16)
POOL_GROUP = BRANCH_WIDTH // len(POOL_WINDOWS)
ATT_HEADS = 8
HEAD_DIM = 64
IDX_HEADS = 8
IDX_DIM = 64
MAX_TOPK = 256
MEM_HEADS = 4
MEM_HEAD_DIM = 128
MEM_WIDTH = MEM_HEADS * MEM_HEAD_DIM
D_FF = 2816
RMS_EPS = 1e-6

LANES = 128
BF16_ROWS = 16
VMEM_LIMIT = 56 * 1024 * 1024

C_LX, C_LG, C_PIN = 0, 512, 1024
C_KK = 1536
C_GL = 1664
D_ROW = C_GL + N_BRANCH * D_MODEL
R_Q, R_QI, R_V, R_W = 0, 512, 1024, 1088
D_TR = R_W + BF16_ROWS
VT_ROWS = HEAD_DIM + BF16_ROWS
NEG_BIG = -1e30
F32_LOWEST = -3.4028234663852886e38


def _params(sem, vmem=VMEM_LIMIT):
    return pltpu.CompilerParams(dimension_semantics=sem, vmem_limit_bytes=vmem)


def _resident(shape):
    nd = len(shape)
    return pl.BlockSpec(shape, lambda *_: (0,) * nd, pipeline_mode=pl.Buffered(1))


def _rms(x, g):
    return x * lax.rsqrt(jnp.mean(x * x, axis=-1, keepdims=True) + RMS_EPS) * g


def _dot_nt(a, b):
    return lax.dot_general(a, b, (((1,), (1,)), ((), ())), preferred_element_type=F32)


def _mix_in_kernel(x_ref, g_ref, w_ref, wt_ref, lx_ref, lg_ref, pin_ref, k_ref, ki_ref, gl_ref,
                   qt_ref, qit_ref, vt_ref, wit_ref):
    h = _rms(x_ref[...], g_ref[...]).astype(BF16)
    tm = h.shape[0]

    def proj(lo, hi):
        return jnp.dot(h, w_ref[:, lo:hi], preferred_element_type=F32)

    lx_ref[...] = proj(C_LX, C_LG)
    lg_ref[...] = proj(C_LG, C_PIN)
    pin_ref[...] = proj(C_PIN, C_KK)
    kk = proj(C_KK, C_GL)
    k_ref[...] = kk[:, :HEAD_DIM].astype(BF16)
    ki_ref[...] = kk[:, HEAD_DIM:].astype(BF16)
    gl_ref[...] = proj(C_GL, D_ROW)

    qt_ref[...] = (_dot_nt(wt_ref[R_Q:R_QI, :], h) * (HEAD_DIM ** -0.5 * math.log2(math.e))).astype(BF16)
    qit_ref[...] = (_dot_nt(wt_ref[R_QI:R_V, :], h) * (IDX_DIM ** -0.5)).astype(BF16)
    vt_ref[0:HEAD_DIM, :] = _dot_nt(wt_ref[R_V:R_W, :], h).astype(BF16)
    pad_row = lax.broadcasted_iota(jnp.int32, (BF16_ROWS, tm), 0)
    vt_ref[HEAD_DIM:VT_ROWS, :] = jnp.where(pad_row == 0, 1.0, 0.0).astype(BF16)
    wit_ref[...] = _dot_nt(wt_ref[R_W:D_TR, :], h)[:IDX_HEADS, :] * (IDX_HEADS ** -0.5)


def _mix_in(xt, g, w, wt, tm, seq):
    t = xt.shape[0]
    nt = t // tm
    nts = seq // tm
    row = lambda n: pl.BlockSpec((tm, n), lambda i: (i, 0))
    col = lambda r: pl.BlockSpec((r, tm), lambda i: (i, 0))
    row_outs = [(512, F32), (512, F32), (512, F32), (HEAD_DIM, BF16), (IDX_DIM, BF16),
                (N_BRANCH * D_MODEL, F32)]
    out_specs = [row(n) for n, _ in row_outs] + [
        col(ATT_HEADS * HEAD_DIM), col(IDX_HEADS * IDX_DIM),
        pl.BlockSpec((VT_ROWS, tm), lambda i: (i // nts, i % nts)), col(IDX_HEADS)]
    out_shape = [jax.ShapeDtypeStruct((t, n), dt) for n, dt in row_outs] + [
        jax.ShapeDtypeStruct((nt * ATT_HEADS * HEAD_DIM, tm), BF16),
        jax.ShapeDtypeStruct((nt * IDX_HEADS * IDX_DIM, tm), BF16),
        jax.ShapeDtypeStruct((t // seq * VT_ROWS, seq), BF16),
        jax.ShapeDtypeStruct((nt * IDX_HEADS, tm), F32)]
    return pl.pallas_call(
        _mix_in_kernel,
        grid=(nt,),
        in_specs=[row(D_MODEL), _resident((1, D_MODEL)), _resident((D_MODEL, D_ROW)),
                  _resident((D_TR, D_MODEL))],
        out_specs=out_specs,
        out_shape=out_shape,
        compiler_params=_params(("parallel",)),
    )(xt, g, w, wt)


CONV_PAD = 8
POOL_PAD = 16


def _log1p(x):
    u = 1.0 + x
    return jnp.where(u == 1.0, x, jnp.log(u) * (x / (u - 1.0)))


def _softplus(x):
    return jnp.maximum(x, 0.0) + _log1p(jnp.exp(-jnp.abs(x)))


def _lru_pool_kernel(lx_ref, lg_ref, pin_ref, cw_ref, cb_ref, wa_ref, ba_ref, wx_ref,
                     bx_ref, ap_ref, wp_ref, ps_ref, ya_ref, yb_ref,
                     xbuf, pbuf, hc_ref, *, ts):
    s = pl.program_id(1)

    @pl.when(s == 0)
    def _():
        xbuf[0:CONV_PAD, :] = jnp.zeros((CONV_PAD, BRANCH_WIDTH), F32)
        pbuf[0:POOL_PAD, :] = jnp.zeros((POOL_PAD, BRANCH_WIDTH), F32)
        hc_ref[...] = jnp.zeros_like(hc_ref)

    xbuf[CONV_PAD:CONV_PAD + ts, :] = lx_ref[...]
    ua = cb_ref[...] + cw_ref[CONV_WIDTH - 1:CONV_WIDTH, :] * xbuf[CONV_PAD:CONV_PAD + ts, :]
    for j in range(CONV_WIDTH - 1):
        off = CONV_PAD - (CONV_WIDTH - 1) + j
        ua = ua + cw_ref[j:j + 1, :] * xbuf[off:off + ts, :]
    xbuf[0:CONV_PAD, :] = xbuf[ts:ts + CONV_PAD, :]

    ub = ua.astype(BF16)
    r = jax.nn.sigmoid(jnp.dot(ub, wa_ref[...], preferred_element_type=F32) + ba_ref[...])
    ig = jax.nn.sigmoid(jnp.dot(ub, wx_ref[...], preferred_element_type=F32) + bx_ref[...])
    log_a = -LRU_C * r * _softplus(-ap_ref[...])
    a = jnp.exp(log_a)
    mult = jnp.sqrt(jnp.maximum(1.0 - jnp.exp(2.0 * log_a), 0.0))
    u = ua * ig * mult

    row = lax.broadcasted_iota(jnp.int32, (ts, LANES), 0)
    for c in range(BRANCH_WIDTH // LANES):
        cs = slice(c * LANES, (c + 1) * LANES)
        ac, hcur = a[:, cs], u[:, cs]
        sh = 1
        while sh < ts:
            keep = row >= sh
            a_sh = jnp.where(keep, pltpu.roll(ac, sh, 0), 1.0)
            h_sh = jnp.where(keep, pltpu.roll(hcur, sh, 0), 0.0)
            hcur = ac * h_sh + hcur
            ac = ac * a_sh
            sh *= 2
        hfull = hcur + ac * hc_ref[:, cs]
        hc_ref[:, cs] = hfull[ts - 1:ts, :]
        ya_ref[:, cs] = (hfull * jax.nn.gelu(lg_ref[:, cs], approximate=True)).astype(BF16)

    pbuf[POOL_PAD:POOL_PAD + ts, :] = pin_ref[...]
    tpos = s * ts + row
    for gi, win in enumerate(POOL_WINDOWS):
        cs = slice(gi * POOL_GROUP, (gi + 1) * POOL_GROUP)
        cur = pbuf[POOL_PAD:POOL_PAD + ts, cs]
        acc = cur
        for j in range(1, win):
            acc = acc + pbuf[POOL_PAD - j:POOL_PAD - j + ts, cs]
        count = jnp.minimum(tpos + 1, win).astype(F32)
        pooled = (acc / count - cur).astype(BF16)
        mixed = jnp.dot(pooled, wp_ref[gi], preferred_element_type=F32)
        yb_ref[:, cs] = (mixed * ps_ref[:, cs]).astype(BF16)
    pbuf[0:POOL_PAD, :] = pbuf[ts:ts + POOL_PAD, :]


def _lru_pool(lx, lg, pin, cw, cb, wa, ba, wx, bx, ap, wp, ps, bsz, seq, ts):
    nt = seq // ts
    tok = pl.BlockSpec((ts, BRANCH_WIDTH), lambda b, s: (b * nt + s, 0))
    vec = _resident((1, BRANCH_WIDTH))
    return pl.pallas_call(
        functools.partial(_lru_pool_kernel, ts=ts),
        grid=(bsz, nt),
        in_specs=[tok, tok, tok, _resident((CONV_WIDTH, BRANCH_WIDTH)), vec,
                  _resident((BRANCH_WIDTH, BRANCH_WIDTH)), vec,
                  _resident((BRANCH_WIDTH, BRANCH_WIDTH)), vec, vec,
                  _resident((len(POOL_WINDOWS), POOL_GROUP, POOL_GROUP)), vec],
        out_specs=[tok, tok],
        out_shape=[jax.ShapeDtypeStruct((bsz * seq, BRANCH_WIDTH), BF16)] * 2,
        scratch_shapes=[pltpu.VMEM((CONV_PAD + ts, BRANCH_WIDTH), F32),
                        pltpu.VMEM((POOL_PAD + ts, BRANCH_WIDTH), F32),
                        pltpu.VMEM((1, BRANCH_WIDTH), F32)],
        compiler_params=_params(("arbitrary", "arbitrary")),
    )(lx, lg, pin, cw, cb, wa, ba, wx, bx, ap, wp, ps)


COUNT_ROWS = 64
REDUCE_ROWS = 32


def _key_to_float(key):
    bits = key ^ (lax.shift_right_arithmetic(key, 31) & jnp.int32(0x7FFFFFFF))
    return lax.bitcast_convert_type(bits, F32)


def _col_reduce(x, op):
    rows, n = x.shape
    return op(op(x.reshape(rows // REDUCE_ROWS, REDUCE_ROWS, n), axis=0), axis=0, keepdims=True)


def _dsa_kernel(qt_ref, qit_ref, wt_ref, k_ref, ki_ref, vt_ref, o_ref, sc_ref,
                *, tq, kc, seq, top_k):
    i = pl.program_id(1)
    nch = ((i + 1) * tq + kc - 1) // kc
    idx_bits = seq.bit_length() - 1

    lane = lax.broadcasted_iota(jnp.int32, (1, tq), 1)
    krow = lax.broadcasted_iota(jnp.int32, (kc, tq), 0)
    qpos = i * tq + lane
    lim = (lax.shift_right_logical(qpos, CHUNK_SHIFT) + 1) * CHUNK

    def idx_body(c, carry):
        off = pl.multiple_of(c * kc, kc)
        kic = ki_ref[pl.ds(off, kc), :]
        acc = None
        for h in range(IDX_HEADS):
            lt = jnp.dot(kic, qit_ref[h * IDX_DIM:(h + 1) * IDX_DIM, :], preferred_element_type=F32)
            term = jnp.maximum(lt, 0.0) * wt_ref[h:h + 1, :]
            acc = term if acc is None else acc + term
        sc_ref[pl.ds(off, kc), :] = jnp.where(off + krow < lim, acc, -jnp.inf)
        return carry

    lax.fori_loop(0, nch, idx_body, 0)
    kca = 2 * kc
    npair = (nch + 1) // 2

    @pl.when(nch % 2 == 1)
    def _():
        sc_ref[pl.ds(pl.multiple_of(nch * kc, kc), kc), :] = jnp.full((kc, tq), -jnp.inf, F32)

    def count(pred):
        def body(c, cnt):
            off = pl.multiple_of(c * kc, kc)
            hit = jnp.where(pred(sc_ref[pl.ds(off, kc), :], off + krow), 1.0, 0.0)
            return cnt + jnp.sum(hit.reshape(kc // COUNT_ROWS, COUNT_ROWS, tq), axis=0)
        cnt = lax.fori_loop(0, nch, body, jnp.zeros((COUNT_ROWS, tq), F32))
        return jnp.sum(cnt, axis=0, keepdims=True)

    kf = jnp.float32(top_k)

    def bit_body(it, carry):
        cur, c_cur = carry
        cand = cur + lax.shift_left(jnp.int32(1), 31 - it)
        pf = _key_to_float(cand)
        cnt = count(lambda x, _: x >= pf)
        ok = cnt >= kf
        return jnp.where(ok, cand, cur), jnp.where(ok, cnt, c_cur)

    cur, c_ge = lax.fori_loop(0, 32, bit_body, (jnp.full((1, tq), -2 ** 31, jnp.int32),
                                                jnp.zeros((1, tq), F32)))
    thr = _key_to_float(cur)
    sel_all = lim <= top_k
    excess = jnp.where(jnp.logical_and(c_ge > kf, jnp.logical_not(sel_all)), 1.0, 0.0)

    def tie_limit():
        need = kf - count(lambda x, _: x > thr)

        def jbit(it, jc):
            cand = jc + lax.shift_left(jnp.int32(1), idx_bits - 1 - it)
            cnt = count(lambda x, kp: jnp.logical_and(x == thr, kp < cand))
            return jnp.where(cnt < need, cand, jc)
        return lax.fori_loop(0, idx_bits, jbit, jnp.zeros((1, tq), jnp.int32))

    jlim = lax.cond(jnp.max(excess) > 0.0, tie_limit,
                    lambda: jnp.full((1, tq), seq, jnp.int32))
    jlim = jnp.where(sel_all, seq, jlim)
    piv_lo = jnp.where(sel_all, F32_LOWEST, thr)
    piv_hi = jnp.where(sel_all, F32_LOWEST, _key_to_float(cur + 1))

    krow2 = lax.broadcasted_iota(jnp.int32, (kca, tq), 0)

    def att_body(c, carry):
        m, accs = carry
        off = pl.multiple_of(c * kca, kca)
        kpos = off + krow2
        pivot = jnp.where(kpos <= jlim, piv_lo, piv_hi)
        bias = jnp.where(sc_ref[pl.ds(off, kca), :] >= pivot, 0.0, -jnp.inf)
        kch = k_ref[pl.ds(off, kca), :]
        vch = vt_ref[:, pl.ds(off, kca)]
        ss = [jnp.dot(kch, qt_ref[h * HEAD_DIM:(h + 1) * HEAD_DIM, :],
                      preferred_element_type=F32) + bias for h in range(ATT_HEADS)]
        m_new = jnp.maximum(m, jnp.concatenate([_col_reduce(s, jnp.max) for s in ss], axis=0))
        alpha = jnp.exp2(m - m_new)
        new_accs = []
        for h in range(ATT_HEADS):
            p = jnp.exp2(ss[h] - m_new[h:h + 1, :]).astype(BF16)
            new_accs.append(alpha[h:h + 1, :] * accs[h]
                            + jnp.dot(vch, p, preferred_element_type=F32))
        return m_new, tuple(new_accs)

    init = (jnp.full((ATT_HEADS, tq), NEG_BIG, F32),
            tuple(jnp.zeros((VT_ROWS, tq), F32) for _ in range(ATT_HEADS)))
    _, accs = lax.fori_loop(0, npair, att_body, init)
    for h in range(ATT_HEADS):
        o_ref[h * HEAD_DIM:(h + 1) * HEAD_DIM, :] = (
            accs[h][0:HEAD_DIM, :] / accs[h][HEAD_DIM:HEAD_DIM + 1, :])


def _dsa(qt, qit, wit, k, ki, vt, bsz, seq, tq, kc):
    nt = seq // tq
    top_k = min(MAX_TOPK, seq // 4)
    tile = lambda r: pl.BlockSpec((r, tq), lambda b, i: (b * nt + i, 0))
    per_batch = lambda r, n: pl.BlockSpec((r, n), lambda b, i: (b, 0))
    return pl.pallas_call(
        functools.partial(_dsa_kernel, tq=tq, kc=kc, seq=seq, top_k=top_k),
        grid=(bsz, nt),
        in_specs=[tile(ATT_HEADS * HEAD_DIM), tile(IDX_HEADS * IDX_DIM), tile(IDX_HEADS),
                  per_batch(seq, HEAD_DIM), per_batch(seq, IDX_DIM), per_batch(VT_ROWS, seq)],
        out_specs=tile(ATT_HEADS * HEAD_DIM),
        out_shape=jax.ShapeDtypeStruct((bsz * nt * ATT_HEADS * HEAD_DIM, tq), F32),
        scratch_shapes=[pltpu.VMEM((seq, tq), F32)],
        compiler_params=_params(("parallel", "arbitrary")),
    )(qt, qit, wit, k, ki, vt)


def _merge_kernel(x_ref, ya_ref, yb_ref, yct_ref, gl_ref, bg_ref, wb_ref, wo_ref, g_ref, o_ref):
    branches = (ya_ref[...], yb_ref[...], yct_ref[...].T.astype(BF16))
    merged = None
    for n, y in enumerate(branches):
        cs = slice(n * D_MODEL, (n + 1) * D_MODEL)
        up = jnp.dot(y, wb_ref[n], preferred_element_type=F32)
        term = jax.nn.sigmoid(gl_ref[:, cs] + bg_ref[:, cs]) * up
        merged = term if merged is None else merged + term
    out = jnp.dot(merged.astype(BF16), wo_ref[...], preferred_element_type=F32)
    o_ref[...] = x_ref[...] + _rms(out, g_ref[...])


def _merge(xt, ya, yb, yct, gl, bg, wb, wo, g, tm):
    t = xt.shape[0]
    row = lambda n: pl.BlockSpec((tm, n), lambda i: (i, 0))
    return pl.pallas_call(
        _merge_kernel,
        grid=(t // tm,),
        in_specs=[row(D_MODEL), row(512), row(512),
                  pl.BlockSpec((ATT_HEADS * HEAD_DIM, tm), lambda i: (i, 0)),
                  row(N_BRANCH * D_MODEL),
                  _resident((1, N_BRANCH * D_MODEL)),
                  _resident((N_BRANCH, BRANCH_WIDTH, D_MODEL)),
                  _resident((D_MODEL, D_MODEL)), _resident((1, D_MODEL))],
        out_specs=row(D_MODEL),
        out_shape=jax.ShapeDtypeStruct((t, D_MODEL), F32),
        compiler_params=_params(("parallel",)),
    )(xt, ya, yb, yct, gl, bg, wb, wo, g)


def _mem_kv_kernel(m_ref, g_ref, w_ref, k_ref, v_ref):
    mn = _rms(m_ref[...], g_ref[...]).astype(BF16)
    k_ref[...] = jnp.dot(mn, w_ref[:, :MEM_WIDTH], preferred_element_type=F32).astype(BF16)
    v_ref[...] = jnp.dot(mn, w_ref[:, MEM_WIDTH:], preferred_element_type=F32).astype(BF16)


def _mem_kv(memt, g, w, bsz):
    blk = lambda n: pl.BlockSpec((N_MEM, n), lambda b: (b, 0))
    return pl.pallas_call(
        _mem_kv_kernel,
        grid=(bsz,),
        in_specs=[blk(D_MODEL), _resident((1, D_MODEL)), _resident((D_MODEL, 2 * MEM_WIDTH))],
        out_specs=[blk(MEM_WIDTH), blk(MEM_WIDTH)],
        out_shape=[jax.ShapeDtypeStruct((bsz * N_MEM, MEM_WIDTH), BF16)] * 2,
        compiler_params=_params(("parallel",)),
    )(memt, g, w)


def _mem_attn_kernel(x_ref, k_ref, v_ref, gp_ref, wq_ref, wo_ref, go_ref, o_ref):
    x = x_ref[...]
    h = _rms(x, gp_ref[...]).astype(BF16)
    q = jnp.dot(h, wq_ref[...], preferred_element_type=F32).astype(BF16)
    heads = []
    for hd in range(MEM_HEADS):
        cs = slice(hd * MEM_HEAD_DIM, (hd + 1) * MEM_HEAD_DIM)
        att = _dot_nt(q[:, cs], k_ref[:, cs]) * (MEM_HEAD_DIM ** -0.5)
        p = jnp.exp(att - jnp.max(att, axis=-1, keepdims=True))
        prob = (p / jnp.sum(p, axis=-1, keepdims=True)).astype(BF16)
        heads.append(jnp.dot(prob, v_ref[:, cs], preferred_element_type=F32).astype(BF16))
    o = jnp.concatenate(heads, axis=-1)
    out = jnp.dot(o, wo_ref[...], preferred_element_type=F32)
    o_ref[...] = x + _rms(out, go_ref[...])


def _mem_attn(xt, kmem, vmem, gp, wq, wo, go, bsz, seq, tm):
    nt = seq // tm
    row = pl.BlockSpec((tm, D_MODEL), lambda b, i: (b * nt + i, 0))
    memblk = pl.BlockSpec((N_MEM, MEM_WIDTH), lambda b, i: (b, 0))
    return pl.pallas_call(
        _mem_attn_kernel,
        grid=(bsz, nt),
        in_specs=[row, memblk, memblk, _resident((1, D_MODEL)),
                  _resident((D_MODEL, MEM_WIDTH)), _resident((MEM_WIDTH, D_MODEL)),
                  _resident((1, D_MODEL))],
        out_specs=row,
        out_shape=jax.ShapeDtypeStruct(xt.shape, F32),
        compiler_params=_params(("parallel", "parallel")),
    )(xt, kmem, vmem, gp, wq, wo, go)


FF_CHUNK = 256


def _ffn_kernel(x_ref, gp_ref, wi_ref, wo_ref, go_ref, o_ref, acc_ref):
    x = x_ref[...]
    h = _rms(x, gp_ref[...]).astype(BF16)
    for c in range(D_FF // FF_CHUNK):
        lo = c * FF_CHUNK
        fg = jnp.dot(h, wi_ref[:, lo:lo + FF_CHUNK], preferred_element_type=F32)
        fu = jnp.dot(h, wi_ref[:, D_FF + lo:D_FF + lo + FF_CHUNK], preferred_element_type=F32)
        act = (fg * jax.nn.sigmoid(fg) * fu).astype(BF16)
        part = jnp.dot(act, wo_ref[lo:lo + FF_CHUNK, :], preferred_element_type=F32)
        if c == 0:
            acc_ref[...] = part
        else:
            acc_ref[...] += part
    o_ref[...] = x + _rms(acc_ref[...], go_ref[...])


def _ffn(xt, gp, wi, wo, go, tm):
    t = xt.shape[0]
    row = pl.BlockSpec((tm, D_MODEL), lambda i: (i, 0))
    return pl.pallas_call(
        _ffn_kernel,
        grid=(t // tm,),
        in_specs=[row, _resident((1, D_MODEL)), _resident((D_MODEL, 2 * D_FF)),
                  _resident((D_FF, D_MODEL)), _resident((1, D_MODEL))],
        out_specs=row,
        out_shape=jax.ShapeDtypeStruct(xt.shape, F32),
        scratch_shapes=[pltpu.VMEM((tm, D_MODEL), F32)],
        compiler_params=_params(("parallel",)),
    )(xt, gp, wi, wo, go)


def _pack_w_in(w):
    o_q = 3 * BRANCH_WIDTH
    o_k = o_q + ATT_HEADS * HEAD_DIM
    o_v = o_k + HEAD_DIM
    o_qi = o_v + HEAD_DIM
    o_ki = o_qi + IDX_HEADS * IDX_DIM
    o_wi = o_ki + IDX_DIM
    o_gl = o_wi + IDX_HEADS
    w_row = jnp.concatenate([w[:, :o_q], w[:, o_k:o_v], w[:, o_ki:o_wi], w[:, o_gl:]], axis=1)
    w_tr = jnp.concatenate([w[:, o_q:o_k], w[:, o_qi:o_ki], w[:, o_v:o_qi], w[:, o_wi:o_gl],
                            jnp.zeros((D_MODEL, D_TR - R_W - IDX_HEADS), w.dtype)], axis=1).T
    return w_row.astype(BF16), w_tr.astype(BF16)


def _block_diag(w):
    nb, n, _ = w.shape
    eye = jnp.eye(nb, dtype=w.dtype)
    return (eye[:, None, :, None] * w[:, :, None, :]).reshape(nb * n, nb * n)


def kernel(x, mem, g_mix_pre, w_in, conv_w, conv_b, lru_w_a, lru_b_a, lru_w_x, lru_b_x,
           lru_a_param, w_pool, pool_scale, w_branch, b_gate, w_out, g_mix_post,
           g_mem_pre, g_mem_kv, w_mem_q, w_mem_kv, w_mem_o, g_mem_post,
           g_ffn_pre, w_ffn_in, w_ffn_out, g_ffn_post):
    bsz, seq, d = x.shape
    assert d == D_MODEL and seq % 256 == 0 and (seq & (seq - 1)) == 0
    t = bsz * seq
    tm = 256
    ts = 256

    vec = lambda a: a.reshape(a.shape[0], 1, -1)
    layers = dict(
        g_mix_pre=vec(g_mix_pre), w_in=w_in, conv_w=conv_w, conv_b=vec(conv_b),
        lru_w_a=lru_w_a, lru_b_a=vec(lru_b_a), lru_w_x=lru_w_x, lru_b_x=vec(lru_b_x),
        lru_a_param=vec(lru_a_param), w_pool=w_pool, pool_scale=vec(pool_scale),
        w_branch=w_branch, b_gate=b_gate.reshape(b_gate.shape[0], 1, -1), w_out=w_out,
        g_mix_post=vec(g_mix_post), g_mem_pre=vec(g_mem_pre), g_mem_kv=vec(g_mem_kv),
        w_mem_q=w_mem_q, w_mem_kv=w_mem_kv, w_mem_o=w_mem_o, g_mem_post=vec(g_mem_post),
        g_ffn_pre=vec(g_ffn_pre), w_ffn_in=w_ffn_in, w_ffn_out=w_ffn_out,
        g_ffn_post=vec(g_ffn_post))
    memt = mem.reshape(bsz * N_MEM, D_MODEL)

    def layer(xt, p):
        w_row, w_tr = _pack_w_in(p["w_in"])
        lx, lg, pin, k, ki, gl, qt, qit, vt, wit = _mix_in(xt, p["g_mix_pre"], w_row, w_tr, tm, seq)
        ya, yb = _lru_pool(lx, lg, pin, p["conv_w"], p["conv_b"],
                           _block_diag(p["lru_w_a"]).astype(BF16), p["lru_b_a"],
                           _block_diag(p["lru_w_x"]).astype(BF16), p["lru_b_x"],
                           p["lru_a_param"], p["w_pool"].astype(BF16), p["pool_scale"],
                           bsz, seq, ts)
        yct = _dsa(qt, qit, wit, k, ki, vt, bsz, seq, tm, tm)
        xt = _merge(xt, ya, yb, yct, gl, p["b_gate"], p["w_branch"].astype(BF16),
                    p["w_out"].astype(BF16), p["g_mix_post"], tm)
        kmem, vmem = _mem_kv(memt, p["g_mem_kv"], p["w_mem_kv"].astype(BF16), bsz)
        xt = _mem_attn(xt, kmem, vmem, p["g_mem_pre"], p["w_mem_q"].astype(BF16),
                       p["w_mem_o"].astype(BF16), p["g_mem_post"], bsz, seq, min(512, seq))
        xt = _ffn(xt, p["g_ffn_pre"], p["w_ffn_in"].astype(BF16),
                  p["w_ffn_out"].astype(BF16), p["g_ffn_post"], min(512, t))
        return xt, None

    xt, _ = lax.scan(layer, x.reshape(t, D_MODEL), layers)
    return xt.reshape(bsz, seq, D_MODEL)
```

```python
import functools
import math

import jax
import jax.numpy as jnp
from jax import lax
from jax.experimental import pallas as pl
from jax.experimental.pallas import tpu as pltpu

F32 = jnp.float32
BF16 = jnp.bfloat16

D_MODEL = 1024
CHUNK = 64
CHUNK_SHIFT = 6
N_MEM = 256
BRANCH_WIDTH = 512
N_BRANCH = 3
LRU_BLOCKS = 8
CONV_WIDTH = 4
LRU_C = 8.0
POOL_WINDOWS = (2, 4, 8, 16)
POOL_GROUP = BRANCH_WIDTH // len(POOL_WINDOWS)
ATT_HEADS = 8
HEAD_DIM = 64
IDX_HEADS = 8
IDX_DIM = 64
MAX_TOPK = 256
MEM_HEADS = 4
MEM_HEAD_DIM = 128
MEM_WIDTH = MEM_HEADS * MEM_HEAD_DIM
D_FF = 2816
RMS_EPS = 1e-6

LANES = 128
SUBLANES = 8
BF16_ROWS = 16
VMEM_LIMIT = 56 * 1024 * 1024

C_LX, C_LG, C_PIN = 0, 512, 1024
C_KK = 1536
C_GL = 1664
D_ROW = C_GL + N_BRANCH * D_MODEL
R_Q, R_QI, R_V, R_W = 0, 512, 1024, 1088
D_TR = R_W + BF16_ROWS
VT_ROWS = HEAD_DIM + BF16_ROWS
NEG_BIG = -1e30
F32_LOWEST = -3.4028234663852886e38


def _params(sem, vmem=VMEM_LIMIT):
    return pltpu.CompilerParams(dimension_semantics=sem, vmem_limit_bytes=vmem)


def _resident(shape):
    nd = len(shape)
    return pl.BlockSpec(shape, lambda *_: (0,) * nd, pipeline_mode=pl.Buffered(1))


def _rms(x, g):
    return x * lax.rsqrt(jnp.mean(x * x, axis=-1, keepdims=True) + RMS_EPS) * g


def _dot_nt(a, b):
    return lax.dot_general(a, b, (((1,), (1,)), ((), ())), preferred_element_type=F32)


CONV_PAD = 8
POOL_PAD = 16


def _log1p(x):
    u = 1.0 + x
    return jnp.where(u == 1.0, x, jnp.log(u) * (x / (u - 1.0)))


def _softplus(x):
    return jnp.maximum(x, 0.0) + _log1p(jnp.exp(-jnp.abs(x)))


def _scan_tile(a, u):
    ts = a.shape[0]
    row = lax.broadcasted_iota(jnp.int32, a.shape, 0)
    sh = 1
    while sh < ts:
        if sh % SUBLANES:
            keep = row >= sh
            a_sh = jnp.where(keep, pltpu.roll(a, sh, 0), 1.0)
            u_sh = jnp.where(keep, pltpu.roll(u, sh, 0), 0.0)
            u = a * u_sh + u
            a = a * a_sh
        else:
            u = jnp.concatenate([u[:sh], a[sh:] * u[:ts - sh] + u[sh:]], axis=0)
            a = jnp.concatenate([a[:sh], a[sh:] * a[:ts - sh]], axis=0)
        sh *= 2
    return a, u


def _lru_mixer(lx, lg, cw_ref, cb_ref, wa_ref, ba_ref, wx_ref, bx_ref, ap_ref, ya_ref,
               xbuf, hc_ref):
    ts = lx.shape[0]
    xbuf[CONV_PAD:CONV_PAD + ts, :] = lx
    ua = cb_ref[...] + cw_ref[CONV_WIDTH - 1:CONV_WIDTH, :] * lx
    for j in range(CONV_WIDTH - 1):
        off = CONV_PAD - (CONV_WIDTH - 1) + j
        ua = ua + cw_ref[j:j + 1, :] * xbuf[off:off + ts, :]
    xbuf[0:CONV_PAD, :] = xbuf[ts:ts + CONV_PAD, :]

    ub = ua.astype(BF16)
    r = jax.nn.sigmoid(jnp.dot(ub, wa_ref[...], preferred_element_type=F32) + ba_ref[...])
    ig = jax.nn.sigmoid(jnp.dot(ub, wx_ref[...], preferred_element_type=F32) + bx_ref[...])
    log_a = -LRU_C * r * _softplus(-ap_ref[...])
    a = jnp.exp(log_a)
    mult = jnp.sqrt(jnp.maximum(1.0 - jnp.exp(2.0 * log_a), 0.0))
    u = ua * ig * mult

    for c in range(BRANCH_WIDTH // LANES):
        cs = slice(c * LANES, (c + 1) * LANES)
        acum, hloc = _scan_tile(a[:, cs], u[:, cs])
        hfull = hloc + acum * hc_ref[:, cs]
        hc_ref[:, cs] = hfull[ts - 1:ts, :]
        ya_ref[:, cs] = (hfull * jax.nn.gelu(lg[:, cs], approximate=True)).astype(BF16)


def _pool_mixer(pin, s, wp_ref, ps_ref, yb_ref, pbuf):
    ts = pin.shape[0]
    pbuf[POOL_PAD:POOL_PAD + ts, :] = pin
    tpos = s * ts + lax.broadcasted_iota(jnp.int32, (ts, POOL_GROUP), 0)
    for gi, win in enumerate(POOL_WINDOWS):
        cs = slice(gi * POOL_GROUP, (gi + 1) * POOL_GROUP)
        cur = pin[:, cs]
        acc = cur
        for j in range(1, win):
            acc = acc + pbuf[POOL_PAD - j:POOL_PAD - j + ts, cs]
        count = jnp.minimum(tpos + 1, win).astype(F32)
        pooled = (acc / count - cur).astype(BF16)
        mixed = jnp.dot(pooled, wp_ref[gi], preferred_element_type=F32)
        yb_ref[:, cs] = (mixed * ps_ref[:, cs]).astype(BF16)
    pbuf[0:POOL_PAD, :] = pbuf[ts:ts + POOL_PAD, :]


def _mix_in_kernel(x_ref, g_ref, w_ref, wt_ref, cw_ref, cb_ref, wa_ref, ba_ref, wx_ref, bx_ref,
                   ap_ref, wp_ref, ps_ref, bg_ref,
                   ya_ref, yb_ref, k_ref, ki_ref, gate_ref, qt_ref, qit_ref, vt_ref, wit_ref,
                   xbuf, pbuf, hc_ref):
    s = pl.program_id(1)
    h = _rms(x_ref[...], g_ref[...]).astype(BF16)
    tm = h.shape[0]

    @pl.when(s == 0)
    def _():
        xbuf[0:CONV_PAD, :] = jnp.zeros((CONV_PAD, BRANCH_WIDTH), F32)
        pbuf[0:POOL_PAD, :] = jnp.zeros((POOL_PAD, BRANCH_WIDTH), F32)
        hc_ref[...] = jnp.zeros_like(hc_ref)

    def proj(lo, hi):
        return jnp.dot(h, w_ref[:, lo:hi], preferred_element_type=F32)

    _lru_mixer(proj(C_LX, C_LG), proj(C_LG, C_PIN), cw_ref, cb_ref, wa_ref, ba_ref,
               wx_ref, bx_ref, ap_ref, ya_ref, xbuf, hc_ref)
    _pool_mixer(proj(C_PIN, C_KK), s, wp_ref, ps_ref, yb_ref, pbuf)

    kk = proj(C_KK, C_GL)
    k_ref[...] = kk[:, :HEAD_DIM].astype(BF16)
    ki_ref[...] = kk[:, HEAD_DIM:].astype(BF16)
    gate_ref[...] = jax.nn.sigmoid(proj(C_GL, D_ROW) + bg_ref[...]).astype(BF16)

    qt_ref[...] = (_dot_nt(wt_ref[R_Q:R_QI, :], h) * (HEAD_DIM ** -0.5 * math.log2(math.e))).astype(BF16)
    qit_ref[...] = (_dot_nt(wt_ref[R_QI:R_V, :], h) * (IDX_DIM ** -0.5)).astype(BF16)
    vt_ref[0:HEAD_DIM, :] = _dot_nt(wt_ref[R_V:R_W, :], h).astype(BF16)
    pad_row = lax.broadcasted_iota(jnp.int32, (BF16_ROWS, tm), 0)
    vt_ref[HEAD_DIM:VT_ROWS, :] = jnp.where(pad_row == 0, 1.0, 0.0).astype(BF16)
    wit_ref[...] = _dot_nt(wt_ref[R_W:D_TR, :], h)[:IDX_HEADS, :] * (IDX_HEADS ** -0.5)


def _mix_in(xt, g, w, wt, cw, cb, wa, ba, wx, bx, ap, wp, ps, bg, bsz, seq, tm):
    t = xt.shape[0]
    nts = seq // tm
    nt = bsz * nts
    row = lambda n: pl.BlockSpec((tm, n), lambda b, s: (b * nts + s, 0))
    col = lambda r: pl.BlockSpec((r, tm), lambda b, s: (b * nts + s, 0))
    vec = _resident((1, BRANCH_WIDTH))
    row_outs = [(BRANCH_WIDTH, BF16), (BRANCH_WIDTH, BF16), (HEAD_DIM, BF16), (IDX_DIM, BF16),
                (N_BRANCH * D_MODEL, BF16)]
    out_specs = [row(n) for n, _ in row_outs] + [
        col(ATT_HEADS * HEAD_DIM), col(IDX_HEADS * IDX_DIM),
        pl.BlockSpec((VT_ROWS, tm), lambda b, s: (b, s)), col(IDX_HEADS)]
    out_shape = [jax.ShapeDtypeStruct((t, n), dt) for n, dt in row_outs] + [
        jax.ShapeDtypeStruct((nt * ATT_HEADS * HEAD_DIM, tm), BF16),
        jax.ShapeDtypeStruct((nt * IDX_HEADS * IDX_DIM, tm), BF16),
        jax.ShapeDtypeStruct((bsz * VT_ROWS, seq), BF16),
        jax.ShapeDtypeStruct((nt * IDX_HEADS, tm), F32)]
    return pl.pallas_call(
        _mix_in_kernel,
        grid=(bsz, nts),
        in_specs=[row(D_MODEL), _resident((1, D_MODEL)), _resident((D_MODEL, D_ROW)),
                  _resident((D_TR, D_MODEL)),
                  _resident((CONV_WIDTH, BRANCH_WIDTH)), vec,
                  _resident((BRANCH_WIDTH, BRANCH_WIDTH)), vec,
                  _resident((BRANCH_WIDTH, BRANCH_WIDTH)), vec, vec,
                  _resident((len(POOL_WINDOWS), POOL_GROUP, POOL_GROUP)), vec,
                  _resident((1, N_BRANCH * D_MODEL))],
        out_specs=out_specs,
        out_shape=out_shape,
        scratch_shapes=[pltpu.VMEM((CONV_PAD + tm, BRANCH_WIDTH), F32),
                        pltpu.VMEM((POOL_PAD + tm, BRANCH_WIDTH), F32),
                        pltpu.VMEM((1, BRANCH_WIDTH), F32)],
        compiler_params=_params(("arbitrary", "arbitrary")),
    )(xt, g, w, wt, cw, cb, wa, ba, wx, bx, ap, wp, ps, bg)


COUNT_ROWS = 64
REDUCE_ROWS = 32


def _key_to_float(key):
    bits = key ^ (lax.shift_right_arithmetic(key, 31) & jnp.int32(0x7FFFFFFF))
    return lax.bitcast_convert_type(bits, F32)


def _col_reduce(x, op):
    rows, n = x.shape
    return op(op(x.reshape(rows // REDUCE_ROWS, REDUCE_ROWS, n), axis=0), axis=0, keepdims=True)


def _dsa_kernel(qt_ref, qit_ref, wt_ref, k_ref, ki_ref, vt_ref, o_ref, sc_ref,
                *, tq, kc, seq, top_k):
    i = pl.program_id(1)
    nch = ((i + 1) * tq + kc - 1) // kc
    idx_bits = seq.bit_length() - 1

    lane = lax.broadcasted_iota(jnp.int32, (1, tq), 1)
    krow = lax.broadcasted_iota(jnp.int32, (kc, tq), 0)
    qpos = i * tq + lane
    lim = (lax.shift_right_logical(qpos, CHUNK_SHIFT) + 1) * CHUNK

    def idx_body(c, carry):
        off = pl.multiple_of(c * kc, kc)
        kic = ki_ref[pl.ds(off, kc), :]
        acc = None
        for h in range(IDX_HEADS):
            lt = jnp.dot(kic, qit_ref[h * IDX_DIM:(h + 1) * IDX_DIM, :], preferred_element_type=F32)
            term = jnp.maximum(lt, 0.0) * wt_ref[h:h + 1, :]
            acc = term if acc is None else acc + term
        sc_ref[pl.ds(off, kc), :] = jnp.where(off + krow < lim, acc, -jnp.inf)
        return carry

    lax.fori_loop(0, nch, idx_body, 0)
    kca = 2 * kc
    npair = (nch + 1) // 2

    @pl.when(nch % 2 == 1)
    def _():
        sc_ref[pl.ds(pl.multiple_of(nch * kc, kc), kc), :] = jnp.full((kc, tq), -jnp.inf, F32)

    def count(pred):
        def body(c, cnt):
            off = pl.multiple_of(c * kc, kc)
            hit = jnp.where(pred(sc_ref[pl.ds(off, kc), :], off + krow), 1.0, 0.0)
            return cnt + jnp.sum(hit.reshape(kc // COUNT_ROWS, COUNT_ROWS, tq), axis=0)
        cnt = lax.fori_loop(0, nch, body, jnp.zeros((COUNT_ROWS, tq), F32))
        return jnp.sum(cnt, axis=0, keepdims=True)

    kf = jnp.float32(top_k)

    def bit_body(it, carry):
        cur, c_cur = carry
        cand = cur + lax.shift_left(jnp.int32(1), 31 - it)
        pf = _key_to_float(cand)
        cnt = count(lambda x, _: x >= pf)
        ok = cnt >= kf
        return jnp.where(ok, cand, cur), jnp.where(ok, cnt, c_cur)

    cur, c_ge = lax.fori_loop(0, 32, bit_body, (jnp.full((1, tq), -2 ** 31, jnp.int32),
                                                jnp.zeros((1, tq), F32)))
    thr = _key_to_float(cur)
    sel_all = lim <= top_k
    excess = jnp.where(jnp.logical_and(c_ge > kf, jnp.logical_not(sel_all)), 1.0, 0.0)

    def tie_limit():
        need = kf - count(lambda x, _: x > thr)

        def jbit(it, jc):
            cand = jc + lax.shift_left(jnp.int32(1), idx_bits - 1 - it)
            cnt = count(lambda x, kp: jnp.logical_and(x == thr, kp < cand))
            return jnp.where(cnt < need, cand, jc)
        return lax.fori_loop(0, idx_bits, jbit, jnp.zeros((1, tq), jnp.int32))

    jlim = lax.cond(jnp.max(excess) > 0.0, tie_limit,
                    lambda: jnp.full((1, tq), seq, jnp.int32))
    jlim = jnp.where(sel_all, seq, jlim)
    piv_lo = jnp.where(sel_all, F32_LOWEST, thr)
    piv_hi = jnp.where(sel_all, F32_LOWEST, _key_to_float(cur + 1))

    krow2 = lax.broadcasted_iota(jnp.int32, (kca, tq), 0)

    def att_body(c, carry):
        m, accs = carry
        off = pl.multiple_of(c * kca, kca)
        kpos = off + krow2
        pivot = jnp.where(kpos <= jlim, piv_lo, piv_hi)
        bias = jnp.where(sc_ref[pl.ds(off, kca), :] >= pivot, 0.0, -jnp.inf)
        kch = k_ref[pl.ds(off, kca), :]
        vch = vt_ref[:, pl.ds(off, kca)]
        ss = [jnp.dot(kch, qt_ref[h * HEAD_DIM:(h + 1) * HEAD_DIM, :],
                      preferred_element_type=F32) + bias for h in range(ATT_HEADS)]
        m_new = jnp.maximum(m, jnp.concatenate([_col_reduce(s, jnp.max) for s in ss], axis=0))
        alpha = jnp.exp2(m - m_new)
        new_accs = []
        for h in range(ATT_HEADS):
            p = jnp.exp2(ss[h] - m_new[h:h + 1, :]).astype(BF16)
            new_accs.append(alpha[h:h + 1, :] * accs[h]
                            + jnp.dot(vch, p, preferred_element_type=F32))
        return m_new, tuple(new_accs)

    init = (jnp.full((ATT_HEADS, tq), NEG_BIG, F32),
            tuple(jnp.zeros((VT_ROWS, tq), F32) for _ in range(ATT_HEADS)))
    _, accs = lax.fori_loop(0, npair, att_body, init)
    for h in range(ATT_HEADS):
        o_ref[h * HEAD_DIM:(h + 1) * HEAD_DIM, :] = (
            accs[h][0:HEAD_DIM, :] / accs[h][HEAD_DIM:HEAD_DIM + 1, :])


def _dsa(qt, qit, wit, k, ki, vt, bsz, seq, tq, kc):
    nt = seq // tq
    top_k = min(MAX_TOPK, seq // 4)
    tile = lambda r: pl.BlockSpec((r, tq), lambda b, i: (b * nt + i, 0))
    per_batch = lambda r, n: pl.BlockSpec((r, n), lambda b, i: (b, 0))
    return pl.pallas_call(
        functools.partial(_dsa_kernel, tq=tq, kc=kc, seq=seq, top_k=top_k),
        grid=(bsz, nt),
        in_specs=[tile(ATT_HEADS * HEAD_DIM), tile(IDX_HEADS * IDX_DIM), tile(IDX_HEADS),
                  per_batch(seq, HEAD_DIM), per_batch(seq, IDX_DIM), per_batch(VT_ROWS, seq)],
        out_specs=tile(ATT_HEADS * HEAD_DIM),
        out_shape=jax.ShapeDtypeStruct((bsz * nt * ATT_HEADS * HEAD_DIM, tq), F32),
        scratch_shapes=[pltpu.VMEM((seq, tq), F32)],
        compiler_params=_params(("parallel", "arbitrary")),
    )(qt, qit, wit, k, ki, vt)


FF_CHUNK = 256


def _mem_kv_kernel(m_ref, g_ref, w_ref, k_ref, v_ref):
    mn = _rms(m_ref[...], g_ref[...]).astype(BF16)
    k_ref[...] = jnp.dot(mn, w_ref[:, :MEM_WIDTH], preferred_element_type=F32).astype(BF16)
    v_ref[...] = jnp.dot(mn, w_ref[:, MEM_WIDTH:], preferred_element_type=F32).astype(BF16)


def _mem_kv(memt, g, w, bsz):
    blk = lambda n: pl.BlockSpec((N_MEM, n), lambda b: (b, 0))
    return pl.pallas_call(
        _mem_kv_kernel,
        grid=(bsz,),
        in_specs=[blk(D_MODEL), _resident((1, D_MODEL)), _resident((D_MODEL, 2 * MEM_WIDTH))],
        out_specs=[blk(MEM_WIDTH), blk(MEM_WIDTH)],
        out_shape=[jax.ShapeDtypeStruct((bsz * N_MEM, MEM_WIDTH), BF16)] * 2,
        compiler_params=_params(("parallel",)),
    )(memt, g, w)


def _post_kernel(x_ref, ya_ref, yb_ref, yct_ref, gate_ref, wb_ref, wo_ref, gmix_ref,
                 km_ref, vm_ref, gmp_ref, wq_ref, wmo_ref, gmo_ref,
                 gfp_ref, wfi_ref, wfo_ref, gfo_ref, o_ref, acc_ref):
    nf = ATT_HEADS * HEAD_DIM
    yc = jnp.concatenate([yct_ref[j * nf:(j + 1) * nf, :].T
                          for j in range(yct_ref.shape[0] // nf)], axis=0).astype(BF16)
    branches = (ya_ref[...], yb_ref[...], yc)
    merged = None
    for n, y in enumerate(branches):
        cs = slice(n * D_MODEL, (n + 1) * D_MODEL)
        term = gate_ref[:, cs] * jnp.dot(y, wb_ref[n], preferred_element_type=F32)
        merged = term if merged is None else merged + term
    out = jnp.dot(merged.astype(BF16), wo_ref[...], preferred_element_type=F32)
    x = x_ref[...] + _rms(out, gmix_ref[...])

    h = _rms(x, gmp_ref[...]).astype(BF16)
    q = jnp.dot(h, wq_ref[...], preferred_element_type=F32).astype(BF16)
    heads = []
    for hd in range(MEM_HEADS):
        cs = slice(hd * MEM_HEAD_DIM, (hd + 1) * MEM_HEAD_DIM)
        att = _dot_nt(q[:, cs], km_ref[:, cs]) * (MEM_HEAD_DIM ** -0.5)
        p = jnp.exp(att - jnp.max(att, axis=-1, keepdims=True))
        prob = (p / jnp.sum(p, axis=-1, keepdims=True)).astype(BF16)
        heads.append(jnp.dot(prob, vm_ref[:, cs], preferred_element_type=F32).astype(BF16))
    out = jnp.dot(jnp.concatenate(heads, axis=-1), wmo_ref[...], preferred_element_type=F32)
    x = x + _rms(out, gmo_ref[...])

    h = _rms(x, gfp_ref[...]).astype(BF16)
    for c in range(D_FF // FF_CHUNK):
        lo = c * FF_CHUNK
        fg = jnp.dot(h, wfi_ref[:, lo:lo + FF_CHUNK], preferred_element_type=F32)
        fu = jnp.dot(h, wfi_ref[:, D_FF + lo:D_FF + lo + FF_CHUNK], preferred_element_type=F32)
        act = (fg * jax.nn.sigmoid(fg) * fu).astype(BF16)
        part = jnp.dot(act, wfo_ref[lo:lo + FF_CHUNK, :], preferred_element_type=F32)
        if c == 0:
            acc_ref[...] = part
        else:
            acc_ref[...] += part
    o_ref[...] = x + _rms(acc_ref[...], gfo_ref[...])


def _post(xt, ya, yb, yct, gates, wb, wo, gmix, kmem, vmem, gmp, wq, wmo, gmo,
          gfp, wfi, wfo, gfo, bsz, seq, tm, tq):
    nts = seq // tm
    row = lambda n: pl.BlockSpec((tm, n), lambda b, i: (b * nts + i, 0))
    memblk = pl.BlockSpec((N_MEM, MEM_WIDTH), lambda b, i: (b, 0))
    gvec = _resident((1, D_MODEL))
    return pl.pallas_call(
        _post_kernel,
        grid=(bsz, nts),
        in_specs=[row(D_MODEL), row(BRANCH_WIDTH), row(BRANCH_WIDTH),
                  pl.BlockSpec((tm // tq * ATT_HEADS * HEAD_DIM, tq), lambda b, i: (b * nts + i, 0)),
                  row(N_BRANCH * D_MODEL),
                  _resident((N_BRANCH, BRANCH_WIDTH, D_MODEL)), _resident((D_MODEL, D_MODEL)), gvec,
                  memblk, memblk, gvec, _resident((D_MODEL, MEM_WIDTH)),
                  _resident((MEM_WIDTH, D_MODEL)), gvec,
                  gvec, _resident((D_MODEL, 2 * D_FF)), _resident((D_FF, D_MODEL)), gvec],
        out_specs=row(D_MODEL),
        out_shape=jax.ShapeDtypeStruct(xt.shape, F32),
        scratch_shapes=[pltpu.VMEM((tm, D_MODEL), F32)],
        compiler_params=_params(("parallel", "parallel")),
    )(xt, ya, yb, yct, gates, wb, wo, gmix, kmem, vmem, gmp, wq, wmo, gmo, gfp, wfi, wfo, gfo)


def _pack_w_in(w):
    o_q = 3 * BRANCH_WIDTH
    o_k = o_q + ATT_HEADS * HEAD_DIM
    o_v = o_k + HEAD_DIM
    o_qi = o_v + HEAD_DIM
    o_ki = o_qi + IDX_HEADS * IDX_DIM
    o_wi = o_ki + IDX_DIM
    o_gl = o_wi + IDX_HEADS
    w_row = jnp.concatenate([w[:, :o_q], w[:, o_k:o_v], w[:, o_ki:o_wi], w[:, o_gl:]], axis=1)
    w_tr = jnp.concatenate([w[:, o_q:o_k], w[:, o_qi:o_ki], w[:, o_v:o_qi], w[:, o_wi:o_gl],
                            jnp.zeros((D_MODEL, D_TR - R_W - IDX_HEADS), w.dtype)], axis=1).T
    return w_row.astype(BF16), w_tr.astype(BF16)


def _block_diag(w):
    nb, n, _ = w.shape
    eye = jnp.eye(nb, dtype=w.dtype)
    return (eye[:, None, :, None] * w[:, :, None, :]).reshape(nb * n, nb * n)


def kernel(x, mem, g_mix_pre, w_in, conv_w, conv_b, lru_w_a, lru_b_a, lru_w_x, lru_b_x,
           lru_a_param, w_pool, pool_scale, w_branch, b_gate, w_out, g_mix_post,
           g_mem_pre, g_mem_kv, w_mem_q, w_mem_kv, w_mem_o, g_mem_post,
           g_ffn_pre, w_ffn_in, w_ffn_out, g_ffn_post):
    bsz, seq, d = x.shape
    assert d == D_MODEL and seq % 512 == 0 and (seq & (seq - 1)) == 0
    t = bsz * seq
    tm = 256

    vec = lambda a: a.reshape(a.shape[0], 1, -1)
    layers = dict(
        g_mix_pre=vec(g_mix_pre), w_in=w_in, conv_w=conv_w, conv_b=vec(conv_b),
        lru_w_a=lru_w_a, lru_b_a=vec(lru_b_a), lru_w_x=lru_w_x, lru_b_x=vec(lru_b_x),
        lru_a_param=vec(lru_a_param), w_pool=w_pool, pool_scale=vec(pool_scale),
        w_branch=w_branch, b_gate=b_gate.reshape(b_gate.shape[0], 1, -1), w_out=w_out,
        g_mix_post=vec(g_mix_post), g_mem_pre=vec(g_mem_pre), g_mem_kv=vec(g_mem_kv),
        w_mem_q=w_mem_q, w_mem_kv=w_mem_kv, w_mem_o=w_mem_o, g_mem_post=vec(g_mem_post),
        g_ffn_pre=vec(g_ffn_pre), w_ffn_in=w_ffn_in, w_ffn_out=w_ffn_out,
        g_ffn_post=vec(g_ffn_post))
    memt = mem.reshape(bsz * N_MEM, D_MODEL)

    def layer(xt, p):
        w_row, w_tr = _pack_w_in(p["w_in"])
        ya, yb, k, ki, gates, qt, qit, vt, wit = _mix_in(
            xt, p["g_mix_pre"], w_row, w_tr, p["conv_w"], p["conv_b"],
            _block_diag(p["lru_w_a"]).astype(BF16), p["lru_b_a"],
            _block_diag(p["lru_w_x"]).astype(BF16), p["lru_b_x"],
            p["lru_a_param"], p["w_pool"].astype(BF16), p["pool_scale"], p["b_gate"],
            bsz, seq, tm)
        yct = _dsa(qt, qit, wit, k, ki, vt, bsz, seq, tm, tm)
        kmem, vmem = _mem_kv(memt, p["g_mem_kv"], p["w_mem_kv"].astype(BF16), bsz)
        xt = _post(xt, ya, yb, yct, gates, p["w_branch"].astype(BF16), p["w_out"].astype(BF16),
                   p["g_mix_post"], kmem, vmem, p["g_mem_pre"], p["w_mem_q"].astype(BF16),
                   p["w_mem_o"].astype(BF16), p["g_mem_post"], p["g_ffn_pre"],
                   p["w_ffn_in"].astype(BF16), p["w_ffn_out"].astype(BF16), p["g_ffn_post"],
                   bsz, seq, 2 * tm, tm)
        return xt, None

    xt, _ = lax.scan(layer, x.reshape(t, D_MODEL), layers)
    return xt.reshape(bsz, seq, D_MODEL)
```

```python
import functools
import math

import jax
import jax.numpy as jnp
from jax import lax
from jax.experimental import pallas as pl
from jax.experimental.pallas import tpu as pltpu

F32 = jnp.float32
BF16 = jnp.bfloat16

D_MODEL = 1024
CHUNK = 64
CHUNK_SHIFT = 6
N_MEM = 256
BRANCH_WIDTH = 512
N_BRANCH = 3
LRU_BLOCKS = 8
CONV_WIDTH = 4
LRU_C = 8.0
POOL_WINDOWS = (2, 4, 8, 16)
POOL_GROUP = BRANCH_WIDTH // len(POOL_WINDOWS)
ATT_HEADS = 8
HEAD_DIM = 64
IDX_HEADS = 8
IDX_DIM = 64
MAX_TOPK = 256
MEM_HEADS = 4
MEM_HEAD_DIM = 128
MEM_WIDTH = MEM_HEADS * MEM_HEAD_DIM
D_FF = 2816
RMS_EPS = 1e-6

LANES = 128
SUBLANES = 8
BF16_ROWS = 16
VMEM_LIMIT = 56 * 1024 * 1024

C_LX, C_LG, C_PIN = 0, 512, 1024
C_KK = 1536
C_GL = 1664
D_ROW = C_GL + N_BRANCH * D_MODEL
R_Q, R_QI, R_V, R_W = 0, 512, 1024, 1088
D_TR = R_W + BF16_ROWS
VT_ROWS = HEAD_DIM + BF16_ROWS
NEG_BIG = -1e30
F32_LOWEST = -3.4028234663852886e38


def _params(sem, vmem=VMEM_LIMIT):
    return pltpu.CompilerParams(dimension_semantics=sem, vmem_limit_bytes=vmem)


def _resident(shape):
    nd = len(shape)
    return pl.BlockSpec(shape, lambda *_: (0,) * nd, pipeline_mode=pl.Buffered(1))


def _rms(x, g):
    return x * lax.rsqrt(jnp.mean(x * x, axis=-1, keepdims=True) + RMS_EPS) * g


def _dot_nt(a, b):
    return lax.dot_general(a, b, (((1,), (1,)), ((), ())), preferred_element_type=F32)


CONV_PAD = 8
POOL_PAD = 16


def _log1p(x):
    u = 1.0 + x
    return jnp.where(u == 1.0, x, jnp.log(u) * (x / (u - 1.0)))


def _softplus(x):
    return jnp.maximum(x, 0.0) + _log1p(jnp.exp(-jnp.abs(x)))


def _scan_tile(a, u):
    ts = a.shape[0]
    row = lax.broadcasted_iota(jnp.int32, a.shape, 0)
    sh = 1
    while sh < ts:
        if sh % SUBLANES:
            keep = row >= sh
            a_sh = jnp.where(keep, pltpu.roll(a, sh, 0), 1.0)
            u_sh = jnp.where(keep, pltpu.roll(u, sh, 0), 0.0)
            u = a * u_sh + u
            a = a * a_sh
        else:
            u = jnp.concatenate([u[:sh], a[sh:] * u[:ts - sh] + u[sh:]], axis=0)
            a = jnp.concatenate([a[:sh], a[sh:] * a[:ts - sh]], axis=0)
        sh *= 2
    return a, u


def _lru_mixer(lx, lg, cw_ref, cb_ref, wa_ref, ba_ref, wx_ref, bx_ref, ap_ref, ya_ref,
               xbuf, hc_ref):
    ts = lx.shape[0]
    xbuf[CONV_PAD:CONV_PAD + ts, :] = lx
    ua = cb_ref[...] + cw_ref[CONV_WIDTH - 1:CONV_WIDTH, :] * lx
    for j in range(CONV_WIDTH - 1):
        off = CONV_PAD - (CONV_WIDTH - 1) + j
        ua = ua + cw_ref[j:j + 1, :] * xbuf[off:off + ts, :]
    xbuf[0:CONV_PAD, :] = xbuf[ts:ts + CONV_PAD, :]

    ub = ua.astype(BF16)
    r = jax.nn.sigmoid(jnp.dot(ub, wa_ref[...], preferred_element_type=F32) + ba_ref[...])
    ig = jax.nn.sigmoid(jnp.dot(ub, wx_ref[...], preferred_element_type=F32) + bx_ref[...])
    log_a = -LRU_C * r * _softplus(-ap_ref[...])
    a = jnp.exp(log_a)
    mult = jnp.sqrt(jnp.maximum(1.0 - jnp.exp(2.0 * log_a), 0.0))
    u = ua * ig * mult

    for c in range(BRANCH_WIDTH // LANES):
        cs = slice(c * LANES, (c + 1) * LANES)
        acum, hloc = _scan_tile(a[:, cs], u[:, cs])
        hfull = hloc + acum * hc_ref[:, cs]
        hc_ref[:, cs] = hfull[ts - 1:ts, :]
        ya_ref[:, cs] = (hfull * jax.nn.gelu(lg[:, cs], approximate=True)).astype(BF16)


def _pool_mixer(pin, s, wp_ref, ps_ref, yb_ref, pbuf):
    ts = pin.shape[0]
    pbuf[POOL_PAD:POOL_PAD + ts, :] = pin
    tpos = s * ts + lax.broadcasted_iota(jnp.int32, (ts, POOL_GROUP), 0)
    for gi, win in enumerate(POOL_WINDOWS):
        cs = slice(gi * POOL_GROUP, (gi + 1) * POOL_GROUP)
        cur = pin[:, cs]
        acc = cur
        for j in range(1, win):
            acc = acc + pbuf[POOL_PAD - j:POOL_PAD - j + ts, cs]
        count = jnp.minimum(tpos + 1, win).astype(F32)
        pooled = (acc / count - cur).astype(BF16)
        mixed = jnp.dot(pooled, wp_ref[gi], preferred_element_type=F32)
        yb_ref[:, cs] = (mixed * ps_ref[:, cs]).astype(BF16)
    pbuf[0:POOL_PAD, :] = pbuf[ts:ts + POOL_PAD, :]


def _mix_in_kernel(x_ref, g_ref, w_ref, wt_ref, cw_ref, cb_ref, wa_ref, ba_ref, wx_ref, bx_ref,
                   ap_ref, wp_ref, ps_ref, bg_ref,
                   ya_ref, yb_ref, k_ref, ki_ref, gate_ref, qt_ref, qit_ref, vt_ref, wit_ref,
                   xbuf, pbuf, hc_ref):
    s = pl.program_id(1)
    h = _rms(x_ref[...], g_ref[...]).astype(BF16)
    tm = h.shape[0]

    @pl.when(s == 0)
    def _():
        xbuf[0:CONV_PAD, :] = jnp.zeros((CONV_PAD, BRANCH_WIDTH), F32)
        pbuf[0:POOL_PAD, :] = jnp.zeros((POOL_PAD, BRANCH_WIDTH), F32)
        hc_ref[...] = jnp.zeros_like(hc_ref)

    def proj(lo, hi):
        return jnp.dot(h, w_ref[:, lo:hi], preferred_element_type=F32)

    _lru_mixer(proj(C_LX, C_LG), proj(C_LG, C_PIN), cw_ref, cb_ref, wa_ref, ba_ref,
               wx_ref, bx_ref, ap_ref, ya_ref, xbuf, hc_ref)
    _pool_mixer(proj(C_PIN, C_KK), s, wp_ref, ps_ref, yb_ref, pbuf)

    kk = proj(C_KK, C_GL)
    k_ref[...] = kk[:, :HEAD_DIM].astype(BF16)
    ki_ref[...] = kk[:, HEAD_DIM:].astype(BF16)
    gate_ref[...] = jax.nn.sigmoid(proj(C_GL, D_ROW) + bg_ref[...]).astype(BF16)

    qt_ref[...] = (_dot_nt(wt_ref[R_Q:R_QI, :], h) * (HEAD_DIM ** -0.5 * math.log2(math.e))).astype(BF16)
    qit_ref[...] = (_dot_nt(wt_ref[R_QI:R_V, :], h) * (IDX_DIM ** -0.5)).astype(BF16)
    vt_ref[0:HEAD_DIM, :] = _dot_nt(wt_ref[R_V:R_W, :], h).astype(BF16)
    pad_row = lax.broadcasted_iota(jnp.int32, (BF16_ROWS, tm), 0)
    vt_ref[HEAD_DIM:VT_ROWS, :] = jnp.where(pad_row == 0, 1.0, 0.0).astype(BF16)
    wit_ref[...] = _dot_nt(wt_ref[R_W:D_TR, :], h)[:IDX_HEADS, :] * (IDX_HEADS ** -0.5)


def _mix_in(xt, g, w, wt, cw, cb, wa, ba, wx, bx, ap, wp, ps, bg, bsz, seq, tm):
    t = xt.shape[0]
    nts = seq // tm
    nt = bsz * nts
    row = lambda n: pl.BlockSpec((tm, n), lambda b, s: (b * nts + s, 0))
    col = lambda r: pl.BlockSpec((r, tm), lambda b, s: (b * nts + s, 0))
    vec = _resident((1, BRANCH_WIDTH))
    row_outs = [(BRANCH_WIDTH, BF16), (BRANCH_WIDTH, BF16), (HEAD_DIM, BF16), (IDX_DIM, BF16),
                (N_BRANCH * D_MODEL, BF16)]
    out_specs = [row(n) for n, _ in row_outs] + [
        col(ATT_HEADS * HEAD_DIM), col(IDX_HEADS * IDX_DIM),
        pl.BlockSpec((VT_ROWS, tm), lambda b, s: (b, s)), col(IDX_HEADS)]
    out_shape = [jax.ShapeDtypeStruct((t, n), dt) for n, dt in row_outs] + [
        jax.ShapeDtypeStruct((nt * ATT_HEADS * HEAD_DIM, tm), BF16),
        jax.ShapeDtypeStruct((nt * IDX_HEADS * IDX_DIM, tm), BF16),
        jax.ShapeDtypeStruct((bsz * VT_ROWS, seq), BF16),
        jax.ShapeDtypeStruct((nt * IDX_HEADS, tm), F32)]
    return pl.pallas_call(
        _mix_in_kernel,
        grid=(bsz, nts),
        in_specs=[row(D_MODEL), _resident((1, D_MODEL)), _resident((D_MODEL, D_ROW)),
                  _resident((D_TR, D_MODEL)),
                  _resident((CONV_WIDTH, BRANCH_WIDTH)), vec,
                  _resident((BRANCH_WIDTH, BRANCH_WIDTH)), vec,
                  _resident((BRANCH_WIDTH, BRANCH_WIDTH)), vec, vec,
                  _resident((len(POOL_WINDOWS), POOL_GROUP, POOL_GROUP)), vec,
                  _resident((1, N_BRANCH * D_MODEL))],
        out_specs=out_specs,
        out_shape=out_shape,
        scratch_shapes=[pltpu.VMEM((CONV_PAD + tm, BRANCH_WIDTH), F32),
                        pltpu.VMEM((POOL_PAD + tm, BRANCH_WIDTH), F32),
                        pltpu.VMEM((1, BRANCH_WIDTH), F32)],
        compiler_params=_params(("arbitrary", "arbitrary")),
    )(xt, g, w, wt, cw, cb, wa, ba, wx, bx, ap, wp, ps, bg)


COUNT_ROWS = 64
REDUCE_ROWS = 32


def _key_to_float(key):
    bits = key ^ (lax.shift_right_arithmetic(key, 31) & jnp.int32(0x7FFFFFFF))
    return lax.bitcast_convert_type(bits, F32)


def _col_reduce(x, op):
    rows, n = x.shape
    return op(op(x.reshape(rows // REDUCE_ROWS, REDUCE_ROWS, n), axis=0), axis=0, keepdims=True)


def _dsa_kernel(qt_ref, qit_ref, wt_ref, k_ref, ki_ref, vt_ref, o_ref, sc_ref,
                *, tq, kc, seq, top_k):
    i = pl.program_id(1)
    nch = ((i + 1) * tq + kc - 1) // kc
    idx_bits = seq.bit_length() - 1

    lane = lax.broadcasted_iota(jnp.int32, (1, tq), 1)
    krow = lax.broadcasted_iota(jnp.int32, (kc, tq), 0)
    qpos = i * tq + lane
    lim = (lax.shift_right_logical(qpos, CHUNK_SHIFT) + 1) * CHUNK

    kca = 2 * kc
    npair = (nch + 1) // 2
    krow2 = lax.broadcasted_iota(jnp.int32, (kca, tq), 0)

    def idx_body(c, carry):
        off = pl.multiple_of(c * kca, kca)
        kic = ki_ref[pl.ds(off, kca), :]
        acc = None
        for h in range(IDX_HEADS):
            lt = jnp.dot(kic, qit_ref[h * IDX_DIM:(h + 1) * IDX_DIM, :], preferred_element_type=F32)
            term = jnp.maximum(lt, 0.0) * wt_ref[h:h + 1, :]
            acc = term if acc is None else acc + term
        sc_ref[pl.ds(off, kca), :] = jnp.where(off + krow2 < lim, acc, -jnp.inf)
        return carry

    lax.fori_loop(0, npair, idx_body, 0)

    def count(pred):
        def body(c, cnt):
            off = pl.multiple_of(c * kc, kc)
            hit = jnp.where(pred(sc_ref[pl.ds(off, kc), :], off + krow), 1.0, 0.0)
            return cnt + jnp.sum(hit.reshape(kc // COUNT_ROWS, COUNT_ROWS, tq), axis=0)
        cnt = lax.fori_loop(0, nch, body, jnp.zeros((COUNT_ROWS, tq), F32))
        return jnp.sum(cnt, axis=0, keepdims=True)

    kf = jnp.float32(top_k)

    def bit_body(it, carry):
        cur, c_cur = carry
        cand = cur + lax.shift_left(jnp.int32(1), 31 - it)
        pf = _key_to_float(cand)
        cnt = count(lambda x, _: x >= pf)
        ok = cnt >= kf
        return jnp.where(ok, cand, cur), jnp.where(ok, cnt, c_cur)

    cur, c_ge = lax.fori_loop(0, 32, bit_body, (jnp.full((1, tq), -2 ** 31, jnp.int32),
                                                jnp.zeros((1, tq), F32)))
    thr = _key_to_float(cur)
    sel_all = lim <= top_k
    excess = jnp.where(jnp.logical_and(c_ge > kf, jnp.logical_not(sel_all)), 1.0, 0.0)

    def tie_limit():
        need = kf - count(lambda x, _: x > thr)

        def jbit(it, jc):
            cand = jc + lax.shift_left(jnp.int32(1), idx_bits - 1 - it)
            cnt = count(lambda x, kp: jnp.logical_and(x == thr, kp < cand))
            return jnp.where(cnt < need, cand, jc)
        return lax.fori_loop(0, idx_bits, jbit, jnp.zeros((1, tq), jnp.int32))

    jlim = lax.cond(jnp.max(excess) > 0.0, tie_limit,
                    lambda: jnp.full((1, tq), seq, jnp.int32))
    jlim = jnp.where(sel_all, seq, jlim)
    piv_lo = jnp.where(sel_all, F32_LOWEST, thr)
    piv_hi = jnp.where(sel_all, F32_LOWEST, _key_to_float(cur + 1))

    def att_body(c, carry):
        m, accs = carry
        off = pl.multiple_of(c * kca, kca)
        kpos = off + krow2
        pivot = jnp.where(kpos <= jlim, piv_lo, piv_hi)
        bias = jnp.where(sc_ref[pl.ds(off, kca), :] >= pivot, 0.0, -jnp.inf)
        kch = k_ref[pl.ds(off, kca), :]
        vch = vt_ref[:, pl.ds(off, kca)]
        ss = [jnp.dot(kch, qt_ref[h * HEAD_DIM:(h + 1) * HEAD_DIM, :],
                      preferred_element_type=F32) + bias for h in range(ATT_HEADS)]
        m_new = jnp.maximum(m, jnp.concatenate([_col_reduce(s, jnp.max) for s in ss], axis=0))
        alpha = jnp.exp2(m - m_new)
        new_accs = []
        for h in range(ATT_HEADS):
            p = jnp.exp2(ss[h] - m_new[h:h + 1, :]).astype(BF16)
            new_accs.append(alpha[h:h + 1, :] * accs[h]
                            + jnp.dot(vch, p, preferred_element_type=F32))
        return m_new, tuple(new_accs)

    init = (jnp.full((ATT_HEADS, tq), NEG_BIG, F32),
            tuple(jnp.zeros((VT_ROWS, tq), F32) for _ in range(ATT_HEADS)))
    _, accs = lax.fori_loop(0, npair, att_body, init)
    for h in range(ATT_HEADS):
        o_ref[h * HEAD_DIM:(h + 1) * HEAD_DIM, :] = (
            accs[h][0:HEAD_DIM, :] / accs[h][HEAD_DIM:HEAD_DIM + 1, :])


def _dsa(qt, qit, wit, k, ki, vt, bsz, seq, tq, kc):
    nt = seq // tq
    top_k = min(MAX_TOPK, seq // 4)
    tile = lambda r: pl.BlockSpec((r, tq), lambda b, i: (b * nt + i, 0))
    per_batch = lambda r, n: pl.BlockSpec((r, n), lambda b, i: (b, 0))
    return pl.pallas_call(
        functools.partial(_dsa_kernel, tq=tq, kc=kc, seq=seq, top_k=top_k),
        grid=(bsz, nt),
        in_specs=[tile(ATT_HEADS * HEAD_DIM), tile(IDX_HEADS * IDX_DIM), tile(IDX_HEADS),
                  per_batch(seq, HEAD_DIM), per_batch(seq, IDX_DIM), per_batch(VT_ROWS, seq)],
        out_specs=tile(ATT_HEADS * HEAD_DIM),
        out_shape=jax.ShapeDtypeStruct((bsz * nt * ATT_HEADS * HEAD_DIM, tq), F32),
        scratch_shapes=[pltpu.VMEM((seq, tq), F32)],
        compiler_params=_params(("parallel", "arbitrary")),
    )(qt, qit, wit, k, ki, vt)


FF_CHUNK = 256


def _mem_kv_kernel(m_ref, g_ref, w_ref, k_ref, v_ref):
    mn = _rms(m_ref[...], g_ref[...]).astype(BF16)
    k_ref[...] = jnp.dot(mn, w_ref[:, :MEM_WIDTH], preferred_element_type=F32).astype(BF16)
    v_ref[...] = jnp.dot(mn, w_ref[:, MEM_WIDTH:], preferred_element_type=F32).astype(BF16)


def _mem_kv(memt, g, w, bsz):
    blk = lambda n: pl.BlockSpec((N_MEM, n), lambda b: (b, 0))
    return pl.pallas_call(
        _mem_kv_kernel,
        grid=(bsz,),
        in_specs=[blk(D_MODEL), _resident((1, D_MODEL)), _resident((D_MODEL, 2 * MEM_WIDTH))],
        out_specs=[blk(MEM_WIDTH), blk(MEM_WIDTH)],
        out_shape=[jax.ShapeDtypeStruct((bsz * N_MEM, MEM_WIDTH), BF16)] * 2,
        compiler_params=_params(("parallel",)),
    )(memt, g, w)


def _post_kernel(x_ref, ya_ref, yb_ref, yct_ref, gate_ref, wb_ref, wo_ref, gmix_ref,
                 km_ref, vm_ref, gmp_ref, wq_ref, wmo_ref, gmo_ref,
                 gfp_ref, wfi_ref, wfo_ref, gfo_ref, o_ref, acc_ref):
    nf = ATT_HEADS * HEAD_DIM
    yc = jnp.concatenate([yct_ref[j * nf:(j + 1) * nf, :].T
                          for j in range(yct_ref.shape[0] // nf)], axis=0).astype(BF16)
    branches = (ya_ref[...], yb_ref[...], yc)
    merged = None
    for n, y in enumerate(branches):
        cs = slice(n * D_MODEL, (n + 1) * D_MODEL)
        term = gate_ref[:, cs] * jnp.dot(y, wb_ref[n], preferred_element_type=F32)
        merged = term if merged is None else merged + term
    out = jnp.dot(merged.astype(BF16), wo_ref[...], preferred_element_type=F32)
    x = x_ref[...] + _rms(out, gmix_ref[...])

    h = _rms(x, gmp_ref[...]).astype(BF16)
    q = jnp.dot(h, wq_ref[...], preferred_element_type=F32).astype(BF16)
    heads = []
    for hd in range(MEM_HEADS):
        cs = slice(hd * MEM_HEAD_DIM, (hd + 1) * MEM_HEAD_DIM)
        att = _dot_nt(q[:, cs], km_ref[:, cs]) * (MEM_HEAD_DIM ** -0.5)
        p = jnp.exp(att - jnp.max(att, axis=-1, keepdims=True))
        prob = (p / jnp.sum(p, axis=-1, keepdims=True)).astype(BF16)
        heads.append(jnp.dot(prob, vm_ref[:, cs], preferred_element_type=F32).astype(BF16))
    out = jnp.dot(jnp.concatenate(heads, axis=-1), wmo_ref[...], preferred_element_type=F32)
    x = x + _rms(out, gmo_ref[...])

    h = _rms(x, gfp_ref[...]).astype(BF16)
    for c in range(D_FF // FF_CHUNK):
        lo = c * FF_CHUNK
        fg = jnp.dot(h, wfi_ref[:, lo:lo + FF_CHUNK], preferred_element_type=F32)
        fu = jnp.dot(h, wfi_ref[:, D_FF + lo:D_FF + lo + FF_CHUNK], preferred_element_type=F32)
        act = (fg * jax.nn.sigmoid(fg) * fu).astype(BF16)
        part = jnp.dot(act, wfo_ref[lo:lo + FF_CHUNK, :], preferred_element_type=F32)
        if c == 0:
            acc_ref[...] = part
        else:
            acc_ref[...] += part
    o_ref[...] = x + _rms(acc_ref[...], gfo_ref[...])


def _post(xt, ya, yb, yct, gates, wb, wo, gmix, kmem, vmem, gmp, wq, wmo, gmo,
          gfp, wfi, wfo, gfo, bsz, seq, tm, tq):
    nts = seq // tm
    row = lambda n: pl.BlockSpec((tm, n), lambda b, i: (b * nts + i, 0))
    memblk = pl.BlockSpec((N_MEM, MEM_WIDTH), lambda b, i: (b, 0))
    gvec = _resident((1, D_MODEL))
    return pl.pallas_call(
        _post_kernel,
        grid=(bsz, nts),
        in_specs=[row(D_MODEL), row(BRANCH_WIDTH), row(BRANCH_WIDTH),
                  pl.BlockSpec((tm // tq * ATT_HEADS * HEAD_DIM, tq), lambda b, i: (b * nts + i, 0)),
                  row(N_BRANCH * D_MODEL),
                  _resident((N_BRANCH, BRANCH_WIDTH, D_MODEL)), _resident((D_MODEL, D_MODEL)), gvec,
                  memblk, memblk, gvec, _resident((D_MODEL, MEM_WIDTH)),
                  _resident((MEM_WIDTH, D_MODEL)), gvec,
                  gvec, _resident((D_MODEL, 2 * D_FF)), _resident((D_FF, D_MODEL)), gvec],
        out_specs=row(D_MODEL),
        out_shape=jax.ShapeDtypeStruct(xt.shape, F32),
        scratch_shapes=[pltpu.VMEM((tm, D_MODEL), F32)],
        input_output_aliases={0: 0},
        compiler_params=_params(("parallel", "parallel")),
    )(xt, ya, yb, yct, gates, wb, wo, gmix, kmem, vmem, gmp, wq, wmo, gmo, gfp, wfi, wfo, gfo)


def _pack_w_in(w):
    o_q = 3 * BRANCH_WIDTH
    o_k = o_q + ATT_HEADS * HEAD_DIM
    o_v = o_k + HEAD_DIM
    o_qi = o_v + HEAD_DIM
    o_ki = o_qi + IDX_HEADS * IDX_DIM
    o_wi = o_ki + IDX_DIM
    o_gl = o_wi + IDX_HEADS
    w_row = jnp.concatenate([w[:, :o_q], w[:, o_k:o_v], w[:, o_ki:o_wi], w[:, o_gl:]], axis=1)
    w_tr = jnp.concatenate([w[:, o_q:o_k], w[:, o_qi:o_ki], w[:, o_v:o_qi], w[:, o_wi:o_gl],
                            jnp.zeros((D_MODEL, D_TR - R_W - IDX_HEADS), w.dtype)], axis=1).T
    return w_row.astype(BF16), w_tr.astype(BF16)


def _block_diag(w):
    nb, n, _ = w.shape
    eye = jnp.eye(nb, dtype=w.dtype)
    return (eye[:, None, :, None] * w[:, :, None, :]).reshape(nb * n, nb * n)


def kernel(x, mem, g_mix_pre, w_in, conv_w, conv_b, lru_w_a, lru_b_a, lru_w_x, lru_b_x,
           lru_a_param, w_pool, pool_scale, w_branch, b_gate, w_out, g_mix_post,
           g_mem_pre, g_mem_kv, w_mem_q, w_mem_kv, w_mem_o, g_mem_post,
           g_ffn_pre, w_ffn_in, w_ffn_out, g_ffn_post):
    bsz, seq, d = x.shape
    assert d == D_MODEL and seq % 512 == 0 and (seq & (seq - 1)) == 0
    t = bsz * seq
    tm = 256

    vec = lambda a: a.reshape(a.shape[0], 1, -1)
    layers = dict(
        g_mix_pre=vec(g_mix_pre), w_in=w_in, conv_w=conv_w, conv_b=vec(conv_b),
        lru_w_a=lru_w_a, lru_b_a=vec(lru_b_a), lru_w_x=lru_w_x, lru_b_x=vec(lru_b_x),
        lru_a_param=vec(lru_a_param), w_pool=w_pool, pool_scale=vec(pool_scale),
        w_branch=w_branch, b_gate=b_gate.reshape(b_gate.shape[0], 1, -1), w_out=w_out,
        g_mix_post=vec(g_mix_post), g_mem_pre=vec(g_mem_pre), g_mem_kv=vec(g_mem_kv),
        w_mem_q=w_mem_q, w_mem_kv=w_mem_kv, w_mem_o=w_mem_o, g_mem_post=vec(g_mem_post),
        g_ffn_pre=vec(g_ffn_pre), w_ffn_in=w_ffn_in, w_ffn_out=w_ffn_out,
        g_ffn_post=vec(g_ffn_post))
    memt = mem.reshape(bsz * N_MEM, D_MODEL)

    def layer(xt, p):
        w_row, w_tr = _pack_w_in(p["w_in"])
        ya, yb, k, ki, gates, qt, qit, vt, wit = _mix_in(
            xt, p["g_mix_pre"], w_row, w_tr, p["conv_w"], p["conv_b"],
            _block_diag(p["lru_w_a"]).astype(BF16), p["lru_b_a"],
            _block_diag(p["lru_w_x"]).astype(BF16), p["lru_b_x"],
            p["lru_a_param"], p["w_pool"].astype(BF16), p["pool_scale"], p["b_gate"],
            bsz, seq, tm)
        yct = _dsa(qt, qit, wit, k, ki, vt, bsz, seq, tm, tm)
        kmem, vmem = _mem_kv(memt, p["g_mem_kv"], p["w_mem_kv"].astype(BF16), bsz)
        xt = _post(xt, ya, yb, yct, gates, p["w_branch"].astype(BF16), p["w_out"].astype(BF16),
                   p["g_mix_post"], kmem, vmem, p["g_mem_pre"], p["w_mem_q"].astype(BF16),
                   p["w_mem_o"].astype(BF16), p["g_mem_post"], p["g_ffn_pre"],
                   p["w_ffn_in"].astype(BF16), p["w_ffn_out"].astype(BF16), p["g_ffn_post"],
                   bsz, seq, 2 * tm, tm)
        return xt, None

    xt, _ = lax.scan(layer, x.reshape(t, D_MODEL), layers)
    return xt.reshape(bsz, seq, D_MODEL)
```

```python
import functools
import math

import jax
import jax.numpy as jnp
from jax import lax
from jax.experimental import pallas as pl
from jax.experimental.pallas import tpu as pltpu

F32 = jnp.float32
BF16 = jnp.bfloat16

D_MODEL = 1024
CHUNK = 64
CHUNK_SHIFT = 6
N_MEM = 256
BRANCH_WIDTH = 512
N_BRANCH = 3
LRU_BLOCKS = 8
CONV_WIDTH = 4
LRU_C = 8.0
POOL_WINDOWS = (2, 4, 8, 16)
POOL_GROUP = BRANCH_WIDTH // len(POOL_WINDOWS)
ATT_HEADS = 8
HEAD_DIM = 64
IDX_HEADS = 8
IDX_DIM = 64
MAX_TOPK = 256
MEM_HEADS = 4
MEM_HEAD_DIM = 128
MEM_WIDTH = MEM_HEADS * MEM_HEAD_DIM
D_FF = 2816
RMS_EPS = 1e-6

LANES = 128
SUBLANES = 8
BF16_ROWS = 16
VMEM_LIMIT = 56 * 1024 * 1024

C_LX, C_LG, C_PIN = 0, 512, 1024
C_KK = 1536
C_GL = 1664
D_ROW = C_GL + N_BRANCH * D_MODEL
R_Q, R_QI, R_V, R_W = 0, 512, 1024, 1088
D_TR = R_W + BF16_ROWS
VT_ROWS = HEAD_DIM + BF16_ROWS
NEG_BIG = -1e30
F32_LOWEST = -3.4028234663852886e38
F32_MIN_NORMAL = 1.1754943508222875e-38


def _params(sem, vmem=VMEM_LIMIT):
    return pltpu.CompilerParams(dimension_semantics=sem, vmem_limit_bytes=vmem)


def _resident(shape):
    nd = len(shape)
    return pl.BlockSpec(shape, lambda *_: (0,) * nd, pipeline_mode=pl.Buffered(1))


def _rms(x, g):
    return x * lax.rsqrt(jnp.mean(x * x, axis=-1, keepdims=True) + RMS_EPS) * g


def _dot_nt(a, b):
    return lax.dot_general(a, b, (((1,), (1,)), ((), ())), preferred_element_type=F32)


CONV_PAD = 8
POOL_PAD = 16


def _log1p(x):
    u = 1.0 + x
    return jnp.where(u == 1.0, x, jnp.log(u) * (x / (u - 1.0)))


def _softplus(x):
    return jnp.maximum(x, 0.0) + _log1p(jnp.exp(-jnp.abs(x)))


def _scan_tile(a, u):
    ts = a.shape[0]
    row = lax.broadcasted_iota(jnp.int32, a.shape, 0)
    sh = 1
    while sh < ts:
        if sh % SUBLANES:
            keep = row >= sh
            a_sh = jnp.where(keep, pltpu.roll(a, sh, 0), 1.0)
            u_sh = jnp.where(keep, pltpu.roll(u, sh, 0), 0.0)
            u = a * u_sh + u
            a = a * a_sh
        else:
            u = jnp.concatenate([u[:sh], a[sh:] * u[:ts - sh] + u[sh:]], axis=0)
            a = jnp.concatenate([a[:sh], a[sh:] * a[:ts - sh]], axis=0)
        sh *= 2
    return a, u


def _lru_mixer(lx, lg, cw_ref, cb_ref, wa_ref, ba_ref, wx_ref, bx_ref, ap_ref, ya_ref,
               xbuf, hc_ref):
    ts = lx.shape[0]
    xbuf[CONV_PAD:CONV_PAD + ts, :] = lx
    ua = cb_ref[...] + cw_ref[CONV_WIDTH - 1:CONV_WIDTH, :] * lx
    for j in range(CONV_WIDTH - 1):
        off = CONV_PAD - (CONV_WIDTH - 1) + j
        ua = ua + cw_ref[j:j + 1, :] * xbuf[off:off + ts, :]
    xbuf[0:CONV_PAD, :] = xbuf[ts:ts + CONV_PAD, :]

    ub = ua.astype(BF16)
    r = jax.nn.sigmoid(jnp.dot(ub, wa_ref[...], preferred_element_type=F32) + ba_ref[...])
    ig = jax.nn.sigmoid(jnp.dot(ub, wx_ref[...], preferred_element_type=F32) + bx_ref[...])
    log_a = -LRU_C * r * _softplus(-ap_ref[...])
    a = jnp.exp(log_a)
    mult = jnp.sqrt(jnp.maximum(1.0 - jnp.exp(2.0 * log_a), 0.0))
    u = ua * ig * mult

    for c in range(BRANCH_WIDTH // LANES):
        cs = slice(c * LANES, (c + 1) * LANES)
        acum, hloc = _scan_tile(a[:, cs], u[:, cs])
        hfull = hloc + acum * hc_ref[:, cs]
        hc_ref[:, cs] = hfull[ts - 1:ts, :]
        ya_ref[:, cs] = (hfull * jax.nn.gelu(lg[:, cs], approximate=True)).astype(BF16)


def _pool_mixer(pin, s, wp_ref, ps_ref, yb_ref, pbuf):
    ts = pin.shape[0]
    pbuf[POOL_PAD:POOL_PAD + ts, :] = pin
    tpos = s * ts + lax.broadcasted_iota(jnp.int32, (ts, POOL_GROUP), 0)
    for gi, win in enumerate(POOL_WINDOWS):
        cs = slice(gi * POOL_GROUP, (gi + 1) * POOL_GROUP)
        cur = pin[:, cs]
        acc = cur
        for j in range(1, win):
            acc = acc + pbuf[POOL_PAD - j:POOL_PAD - j + ts, cs]
        count = jnp.minimum(tpos + 1, win).astype(F32)
        pooled = (acc / count - cur).astype(BF16)
        mixed = jnp.dot(pooled, wp_ref[gi], preferred_element_type=F32)
        yb_ref[:, cs] = (mixed * ps_ref[:, cs]).astype(BF16)
    pbuf[0:POOL_PAD, :] = pbuf[ts:ts + POOL_PAD, :]


def _mix_in_kernel(x_ref, g_ref, w_ref, wt_ref, cw_ref, cb_ref, wa_ref, ba_ref, wx_ref, bx_ref,
                   ap_ref, wp_ref, ps_ref, bg_ref,
                   ya_ref, yb_ref, k_ref, ki_ref, gate_ref, qt_ref, qit_ref, vt_ref, wit_ref,
                   xbuf, pbuf, hc_ref):
    s = pl.program_id(1)
    h = _rms(x_ref[...], g_ref[...]).astype(BF16)
    tm = h.shape[0]

    @pl.when(s == 0)
    def _():
        xbuf[0:CONV_PAD, :] = jnp.zeros((CONV_PAD, BRANCH_WIDTH), F32)
        pbuf[0:POOL_PAD, :] = jnp.zeros((POOL_PAD, BRANCH_WIDTH), F32)
        hc_ref[...] = jnp.zeros_like(hc_ref)

    def proj(lo, hi):
        return jnp.dot(h, w_ref[:, lo:hi], preferred_element_type=F32)

    lx, lg, pin = proj(C_LX, C_LG), proj(C_LG, C_PIN), proj(C_PIN, C_KK)
    kk = proj(C_KK, C_GL)
    k_ref[...] = kk[:, :HEAD_DIM].astype(BF16)
    ki_ref[...] = kk[:, HEAD_DIM:].astype(BF16)
    gate_ref[...] = jax.nn.sigmoid(proj(C_GL, D_ROW) + bg_ref[...]).astype(BF16)

    qt_ref[...] = (_dot_nt(wt_ref[R_Q:R_QI, :], h) * (HEAD_DIM ** -0.5 * math.log2(math.e))).astype(BF16)
    qit_ref[...] = (_dot_nt(wt_ref[R_QI:R_V, :], h) * (IDX_DIM ** -0.5)).astype(BF16)
    vt_ref[0:HEAD_DIM, :] = _dot_nt(wt_ref[R_V:R_W, :], h).astype(BF16)
    pad_row = lax.broadcasted_iota(jnp.int32, (BF16_ROWS, tm), 0)
    vt_ref[HEAD_DIM:VT_ROWS, :] = jnp.where(pad_row == 0, 1.0, 0.0).astype(BF16)
    wit_ref[...] = _dot_nt(wt_ref[R_W:D_TR, :], h)[:IDX_HEADS, :] * (IDX_HEADS ** -0.5)

    _lru_mixer(lx, lg, cw_ref, cb_ref, wa_ref, ba_ref, wx_ref, bx_ref, ap_ref, ya_ref,
               xbuf, hc_ref)
    _pool_mixer(pin, s, wp_ref, ps_ref, yb_ref, pbuf)


def _mix_in(xt, g, w, wt, cw, cb, wa, ba, wx, bx, ap, wp, ps, bg, bsz, seq, tm):
    t = xt.shape[0]
    nts = seq // tm
    nt = bsz * nts
    row = lambda n: pl.BlockSpec((tm, n), lambda b, s: (b * nts + s, 0))
    col = lambda r: pl.BlockSpec((r, tm), lambda b, s: (b * nts + s, 0))
    vec = _resident((1, BRANCH_WIDTH))
    row_outs = [(BRANCH_WIDTH, BF16), (BRANCH_WIDTH, BF16), (HEAD_DIM, BF16), (IDX_DIM, BF16),
                (N_BRANCH * D_MODEL, BF16)]
    out_specs = [row(n) for n, _ in row_outs] + [
        col(ATT_HEADS * HEAD_DIM), col(IDX_HEADS * IDX_DIM),
        pl.BlockSpec((VT_ROWS, tm), lambda b, s: (b, s)), col(IDX_HEADS)]
    out_shape = [jax.ShapeDtypeStruct((t, n), dt) for n, dt in row_outs] + [
        jax.ShapeDtypeStruct((nt * ATT_HEADS * HEAD_DIM, tm), BF16),
        jax.ShapeDtypeStruct((nt * IDX_HEADS * IDX_DIM, tm), BF16),
        jax.ShapeDtypeStruct((bsz * VT_ROWS, seq), BF16),
        jax.ShapeDtypeStruct((nt * IDX_HEADS, tm), F32)]
    return pl.pallas_call(
        _mix_in_kernel,
        grid=(bsz, nts),
        in_specs=[row(D_MODEL), _resident((1, D_MODEL)), _resident((D_MODEL, D_ROW)),
                  _resident((D_TR, D_MODEL)),
                  _resident((CONV_WIDTH, BRANCH_WIDTH)), vec,
                  _resident((BRANCH_WIDTH, BRANCH_WIDTH)), vec,
                  _resident((BRANCH_WIDTH, BRANCH_WIDTH)), vec, vec,
                  _resident((len(POOL_WINDOWS), POOL_GROUP, POOL_GROUP)), vec,
                  _resident((1, N_BRANCH * D_MODEL))],
        out_specs=out_specs,
        out_shape=out_shape,
        scratch_shapes=[pltpu.VMEM((CONV_PAD + tm, BRANCH_WIDTH), F32),
                        pltpu.VMEM((POOL_PAD + tm, BRANCH_WIDTH), F32),
                        pltpu.VMEM((1, BRANCH_WIDTH), F32)],
        compiler_params=_params(("arbitrary", "arbitrary")),
    )(xt, g, w, wt, cw, cb, wa, ba, wx, bx, ap, wp, ps, bg)


COUNT_ROWS = 64
REDUCE_ROWS = 32


def _key_to_float(key):
    bits = key ^ (lax.shift_right_arithmetic(key, 31) & jnp.int32(0x7FFFFFFF))
    return lax.bitcast_convert_type(bits, F32)


def _float_to_key(x):
    bits = lax.bitcast_convert_type(x, jnp.int32)
    return bits ^ (lax.shift_right_arithmetic(bits, 31) & jnp.int32(0x7FFFFFFF))


def _col_reduce(x, op):
    rows, n = x.shape
    return op(op(x.reshape(rows // REDUCE_ROWS, REDUCE_ROWS, n), axis=0), axis=0, keepdims=True)


def _dsa_kernel(qt_ref, qit_ref, wt_ref, k_ref, ki_ref, vt_ref, o_ref, sc_ref, scb_ref,
                *, tq, kc, seq, top_k):
    i = pl.program_id(1)
    nch = ((i + 1) * tq + kc - 1) // kc
    idx_bits = seq.bit_length() - 1

    lane = lax.broadcasted_iota(jnp.int32, (1, tq), 1)
    krow = lax.broadcasted_iota(jnp.int32, (kc, tq), 0)
    qpos = i * tq + lane
    lim = (lax.shift_right_logical(qpos, CHUNK_SHIFT) + 1) * CHUNK

    kca = 2 * kc
    npair = (nch + 1) // 2
    krow2 = lax.broadcasted_iota(jnp.int32, (kca, tq), 0)

    def idx_body(c, carry):
        off = pl.multiple_of(c * kca, kca)
        kic = ki_ref[pl.ds(off, kca), :]
        acc = None
        for h in range(IDX_HEADS):
            lt = jnp.dot(kic, qit_ref[h * IDX_DIM:(h + 1) * IDX_DIM, :], preferred_element_type=F32)
            term = jnp.maximum(lt, 0.0) * wt_ref[h:h + 1, :]
            acc = term if acc is None else acc + term
        sc = jnp.where(off + krow2 < lim, acc, -jnp.inf)
        sc_ref[pl.ds(off, kca), :] = sc
        scb_ref[pl.ds(off, kca), :] = sc.astype(jnp.bfloat16)
        return carry

    lax.fori_loop(0, npair, idx_body, 0)

    def count(pred):
        def body(c, cnt):
            off = pl.multiple_of(c * kc, kc)
            hit = jnp.where(pred(sc_ref[pl.ds(off, kc), :], off + krow), 1.0, 0.0)
            return cnt + jnp.sum(hit.reshape(kc // COUNT_ROWS, COUNT_ROWS, tq), axis=0)
        cnt = lax.fori_loop(0, nch, body, jnp.zeros((COUNT_ROWS, tq), F32))
        return jnp.sum(cnt, axis=0, keepdims=True)

    kf = jnp.float32(top_k)

    def count16(pb):
        def body(c, cnt):
            off = pl.multiple_of(c * kc, kc)
            hit = jnp.where(scb_ref[pl.ds(off, kc), :] >= pb, ONE16, ZERO16)
            for j in range(kc // COUNT_ROWS):
                cnt = cnt + hit[j * COUNT_ROWS:(j + 1) * COUNT_ROWS]
            return cnt
        cnt = lax.fori_loop(0, nch, body, jnp.zeros((COUNT_ROWS, tq), jnp.bfloat16))
        return jnp.sum(cnt.astype(F32), axis=0, keepdims=True)

    ONE16, ZERO16 = jnp.ones((), jnp.bfloat16), jnp.zeros((), jnp.bfloat16)

    def bit16_body(it, cur):
        cand = cur + lax.shift_left(jnp.int32(1), 15 - it)
        half = cand ^ (lax.shift_right_arithmetic(cand, 31) & jnp.int32(0x7FFF))
        pb = lax.bitcast_convert_type(lax.shift_left(half, 16), F32).astype(jnp.bfloat16)
        return jnp.where(count16(pb) >= kf, cand, cur)

    cur16 = lax.fori_loop(0, 16, bit16_body, jnp.full((1, tq), -2 ** 15, jnp.int32))
    half = cur16 ^ (lax.shift_right_arithmetic(cur16, 31) & jnp.int32(0x7FFF))
    bucket_key = _float_to_key(lax.bitcast_convert_type(lax.shift_left(half, 16), F32))

    top = bucket_key + (2 ** 15 + 2)

    def bit_body(it, o):
        cand = o + lax.shift_left(jnp.int32(1), 16 - it)
        pf = _key_to_float(top - cand)
        return jnp.where(count(lambda x, _: x >= pf) < kf, cand, o)

    cur = top - lax.fori_loop(0, 17, bit_body, jnp.zeros((1, tq), jnp.int32)) - 1
    thr = _key_to_float(cur)
    nxt = _key_to_float(cur + 1)
    tiny = jnp.abs(thr) < F32_MIN_NORMAL
    thr = jnp.where(tiny, 0.0, thr)
    nxt = jnp.where(tiny, F32_MIN_NORMAL, nxt)
    c_ge = count(lambda x, _: x >= thr)
    sel_all = lim <= top_k
    excess = jnp.where(jnp.logical_and(c_ge > kf, jnp.logical_not(sel_all)), 1.0, 0.0)

    def tie_limit():
        need = kf - count(lambda x, _: x > thr)

        def jbit(it, jc):
            cand = jc + lax.shift_left(jnp.int32(1), idx_bits - 1 - it)
            cnt = count(lambda x, kp: jnp.logical_and(x == thr, kp < cand))
            return jnp.where(cnt < need, cand, jc)
        return lax.fori_loop(0, idx_bits, jbit, jnp.zeros((1, tq), jnp.int32))

    jlim = lax.cond(jnp.max(excess) > 0.0, tie_limit,
                    lambda: jnp.full((1, tq), seq, jnp.int32))
    jlim = jnp.where(sel_all, seq, jlim)
    piv_lo = jnp.where(sel_all, F32_LOWEST, thr)
    piv_hi = jnp.where(sel_all, F32_LOWEST, nxt)

    def att_body(c, carry):
        m, accs = carry
        off = pl.multiple_of(c * kca, kca)
        kpos = off + krow2
        pivot = jnp.where(kpos <= jlim, piv_lo, piv_hi)
        bias = jnp.where(sc_ref[pl.ds(off, kca), :] >= pivot, 0.0, -jnp.inf)
        kch = k_ref[pl.ds(off, kca), :]
        vch = vt_ref[:, pl.ds(off, kca)]
        ss = [jnp.dot(kch, qt_ref[h * HEAD_DIM:(h + 1) * HEAD_DIM, :],
                      preferred_element_type=F32) + bias for h in range(ATT_HEADS)]
        m_new = jnp.maximum(m, jnp.concatenate([_col_reduce(s, jnp.max) for s in ss], axis=0))
        alpha = jnp.exp2(m - m_new)
        new_accs = []
        for h in range(ATT_HEADS):
            p = jnp.exp2(ss[h] - m_new[h:h + 1, :]).astype(BF16)
            new_accs.append(alpha[h:h + 1, :] * accs[h]
                            + jnp.dot(vch, p, preferred_element_type=F32))
        return m_new, tuple(new_accs)

    init = (jnp.full((ATT_HEADS, tq), NEG_BIG, F32),
            tuple(jnp.zeros((VT_ROWS, tq), F32) for _ in range(ATT_HEADS)))
    _, accs = lax.fori_loop(0, npair, att_body, init)
    for h in range(ATT_HEADS):
        o_ref[h * HEAD_DIM:(h + 1) * HEAD_DIM, :] = (
            accs[h][0:HEAD_DIM, :] / accs[h][HEAD_DIM:HEAD_DIM + 1, :])


def _dsa(qt, qit, wit, k, ki, vt, bsz, seq, tq, kc):
    nt = seq // tq
    top_k = min(MAX_TOPK, seq // 4)
    assert seq // COUNT_ROWS <= 256
    tile = lambda r: pl.BlockSpec((r, tq), lambda b, i: (b * nt + i, 0))
    per_batch = lambda r, n: pl.BlockSpec((r, n), lambda b, i: (b, 0))
    return pl.pallas_call(
        functools.partial(_dsa_kernel, tq=tq, kc=kc, seq=seq, top_k=top_k),
        grid=(bsz, nt),
        in_specs=[tile(ATT_HEADS * HEAD_DIM), tile(IDX_HEADS * IDX_DIM), tile(IDX_HEADS),
                  per_batch(seq, HEAD_DIM), per_batch(seq, IDX_DIM), per_batch(VT_ROWS, seq)],
        out_specs=tile(ATT_HEADS * HEAD_DIM),
        out_shape=jax.ShapeDtypeStruct((bsz * nt * ATT_HEADS * HEAD_DIM, tq), F32),
        scratch_shapes=[pltpu.VMEM((seq, tq), F32), pltpu.VMEM((seq, tq), jnp.bfloat16)],
        compiler_params=_params(("parallel", "arbitrary")),
    )(qt, qit, wit, k, ki, vt)


FF_CHUNK = 256


def _mem_kv_kernel(m_ref, g_ref, w_ref, k_ref, v_ref):
    mn = _rms(m_ref[...], g_ref[...]).astype(BF16)
    k_ref[...] = jnp.dot(mn, w_ref[:, :MEM_WIDTH], preferred_element_type=F32).astype(BF16)
    v_ref[...] = jnp.dot(mn, w_ref[:, MEM_WIDTH:], preferred_element_type=F32).astype(BF16)


def _mem_kv(memt, g, w, bsz):
    blk = lambda n: pl.BlockSpec((N_MEM, n), lambda b: (b, 0))
    return pl.pallas_call(
        _mem_kv_kernel,
        grid=(bsz,),
        in_specs=[blk(D_MODEL), _resident((1, D_MODEL)), _resident((D_MODEL, 2 * MEM_WIDTH))],
        out_specs=[blk(MEM_WIDTH), blk(MEM_WIDTH)],
        out_shape=[jax.ShapeDtypeStruct((bsz * N_MEM, MEM_WIDTH), BF16)] * 2,
        compiler_params=_params(("parallel",)),
    )(memt, g, w)


def _post_kernel(x_ref, ya_ref, yb_ref, yct_ref, gate_ref, wb_ref, wo_ref, gmix_ref,
                 km_ref, vm_ref, gmp_ref, wq_ref, wmo_ref, gmo_ref,
                 gfp_ref, wfi_ref, wfo_ref, gfo_ref, o_ref, acc_ref):
    nf = ATT_HEADS * HEAD_DIM
    yc = jnp.concatenate([yct_ref[j * nf:(j + 1) * nf, :].T
                          for j in range(yct_ref.shape[0] // nf)], axis=0).astype(BF16)
    branches = (ya_ref[...], yb_ref[...], yc)
    merged = None
    for n, y in enumerate(branches):
        cs = slice(n * D_MODEL, (n + 1) * D_MODEL)
        term = gate_ref[:, cs] * jnp.dot(y, wb_ref[n], preferred_element_type=F32)
        merged = term if merged is None else merged + term
    out = jnp.dot(merged.astype(BF16), wo_ref[...], preferred_element_type=F32)
    x = x_ref[...] + _rms(out, gmix_ref[...])

    h = _rms(x, gmp_ref[...]).astype(BF16)
    q = jnp.dot(h, wq_ref[...], preferred_element_type=F32).astype(BF16)
    heads = []
    for hd in range(MEM_HEADS):
        cs = slice(hd * MEM_HEAD_DIM, (hd + 1) * MEM_HEAD_DIM)
        att = _dot_nt(q[:, cs], km_ref[:, cs]) * (MEM_HEAD_DIM ** -0.5)
        p = jnp.exp(att - jnp.max(att, axis=-1, keepdims=True))
        prob = (p / jnp.sum(p, axis=-1, keepdims=True)).astype(BF16)
        heads.append(jnp.dot(prob, vm_ref[:, cs], preferred_element_type=F32).astype(BF16))
    out = jnp.dot(jnp.concatenate(heads, axis=-1), wmo_ref[...], preferred_element_type=F32)
    x = x + _rms(out, gmo_ref[...])

    h = _rms(x, gfp_ref[...]).astype(BF16)
    for c in range(D_FF // FF_CHUNK):
        lo = c * FF_CHUNK
        fg = jnp.dot(h, wfi_ref[:, lo:lo + FF_CHUNK], preferred_element_type=F32)
        fu = jnp.dot(h, wfi_ref[:, D_FF + lo:D_FF + lo + FF_CHUNK], preferred_element_type=F32)
        act = (fg * jax.nn.sigmoid(fg) * fu).astype(BF16)
        part = jnp.dot(act, wfo_ref[lo:lo + FF_CHUNK, :], preferred_element_type=F32)
        if c == 0:
            acc_ref[...] = part
        else:
            acc_ref[...] += part
    o_ref[...] = x + _rms(acc_ref[...], gfo_ref[...])


def _post(xt, ya, yb, yct, gates, wb, wo, gmix, kmem, vmem, gmp, wq, wmo, gmo,
          gfp, wfi, wfo, gfo, bsz, seq, tm, tq):
    nts = seq // tm
    row = lambda n: pl.BlockSpec((tm, n), lambda b, i: (b * nts + i, 0))
    memblk = pl.BlockSpec((N_MEM, MEM_WIDTH), lambda b, i: (b, 0))
    gvec = _resident((1, D_MODEL))
    return pl.pallas_call(
        _post_kernel,
        grid=(bsz, nts),
        in_specs=[row(D_MODEL), row(BRANCH_WIDTH), row(BRANCH_WIDTH),
                  pl.BlockSpec((tm // tq * ATT_HEADS * HEAD_DIM, tq), lambda b, i: (b * nts + i, 0)),
                  row(N_BRANCH * D_MODEL),
                  _resident((N_BRANCH, BRANCH_WIDTH, D_MODEL)), _resident((D_MODEL, D_MODEL)), gvec,
                  memblk, memblk, gvec, _resident((D_MODEL, MEM_WIDTH)),
                  _resident((MEM_WIDTH, D_MODEL)), gvec,
                  gvec, _resident((D_MODEL, 2 * D_FF)), _resident((D_FF, D_MODEL)), gvec],
        out_specs=row(D_MODEL),
        out_shape=jax.ShapeDtypeStruct(xt.shape, F32),
        scratch_shapes=[pltpu.VMEM((tm, D_MODEL), F32)],
        input_output_aliases={0: 0},
        compiler_params=_params(("parallel", "parallel")),
    )(xt, ya, yb, yct, gates, wb, wo, gmix, kmem, vmem, gmp, wq, wmo, gmo, gfp, wfi, wfo, gfo)


def _pack_w_in(w):
    o_q = 3 * BRANCH_WIDTH
    o_k = o_q + ATT_HEADS * HEAD_DIM
    o_v = o_k + HEAD_DIM
    o_qi = o_v + HEAD_DIM
    o_ki = o_qi + IDX_HEADS * IDX_DIM
    o_wi = o_ki + IDX_DIM
    o_gl = o_wi + IDX_HEADS
    w_row = jnp.concatenate([w[:, :o_q], w[:, o_k:o_v], w[:, o_ki:o_wi], w[:, o_gl:]], axis=1)
    w_tr = jnp.concatenate([w[:, o_q:o_k], w[:, o_qi:o_ki], w[:, o_v:o_qi], w[:, o_wi:o_gl],
                            jnp.zeros((D_MODEL, D_TR - R_W - IDX_HEADS), w.dtype)], axis=1).T
    return w_row.astype(BF16), w_tr.astype(BF16)


def _block_diag(w):
    nb, n, _ = w.shape
    eye = jnp.eye(nb, dtype=w.dtype)
    return (eye[:, None, :, None] * w[:, :, None, :]).reshape(nb * n, nb * n)


def kernel(x, mem, g_mix_pre, w_in, conv_w, conv_b, lru_w_a, lru_b_a, lru_w_x, lru_b_x,
           lru_a_param, w_pool, pool_scale, w_branch, b_gate, w_out, g_mix_post,
           g_mem_pre, g_mem_kv, w_mem_q, w_mem_kv, w_mem_o, g_mem_post,
           g_ffn_pre, w_ffn_in, w_ffn_out, g_ffn_post):
    bsz, seq, d = x.shape
    assert d == D_MODEL and seq % 512 == 0 and (seq & (seq - 1)) == 0
    t = bsz * seq
    tm = 256

    vec = lambda a: a.reshape(a.shape[0], 1, -1)
    layers = dict(
        g_mix_pre=vec(g_mix_pre), w_in=w_in, conv_w=conv_w, conv_b=vec(conv_b),
        lru_w_a=lru_w_a, lru_b_a=vec(lru_b_a), lru_w_x=lru_w_x, lru_b_x=vec(lru_b_x),
        lru_a_param=vec(lru_a_param), w_pool=w_pool, pool_scale=vec(pool_scale),
        w_branch=w_branch, b_gate=b_gate.reshape(b_gate.shape[0], 1, -1), w_out=w_out,
        g_mix_post=vec(g_mix_post), g_mem_pre=vec(g_mem_pre), g_mem_kv=vec(g_mem_kv),
        w_mem_q=w_mem_q, w_mem_kv=w_mem_kv, w_mem_o=w_mem_o, g_mem_post=vec(g_mem_post),
        g_ffn_pre=vec(g_ffn_pre), w_ffn_in=w_ffn_in, w_ffn_out=w_ffn_out,
        g_ffn_post=vec(g_ffn_post))
    memt = mem.reshape(bsz * N_MEM, D_MODEL)

    def layer(xt, p):
        w_row, w_tr = _pack_w_in(p["w_in"])
        ya, yb, k, ki, gates, qt, qit, vt, wit = _mix_in(
            xt, p["g_mix_pre"], w_row, w_tr, p["conv_w"], p["conv_b"],
            _block_diag(p["lru_w_a"]).astype(BF16), p["lru_b_a"],
            _block_diag(p["lru_w_x"]).astype(BF16), p["lru_b_x"],
            p["lru_a_param"], p["w_pool"].astype(BF16), p["pool_scale"], p["b_gate"],
            bsz, seq, tm)
        yct = _dsa(qt, qit, wit, k, ki, vt, bsz, seq, tm, tm)
        kmem, vmem = _mem_kv(memt, p["g_mem_kv"], p["w_mem_kv"].astype(BF16), bsz)
        xt = _post(xt, ya, yb, yct, gates, p["w_branch"].astype(BF16), p["w_out"].astype(BF16),
                   p["g_mix_post"], kmem, vmem, p["g_mem_pre"], p["w_mem_q"].astype(BF16),
                   p["w_mem_o"].astype(BF16), p["g_mem_post"], p["g_ffn_pre"],
                   p["w_ffn_in"].astype(BF16), p["w_ffn_out"].astype(BF16), p["g_ffn_post"],
                   bsz, seq, 2 * tm, tm)
        return xt, None

    xt, _ = lax.scan(layer, x.reshape(t, D_MODEL), layers)
    return xt.reshape(bsz, seq, D_MODEL)
```

```python
import functools
import math

import jax
import jax.numpy as jnp
from jax import lax
from jax.experimental import pallas as pl
from jax.experimental.pallas import tpu as pltpu

F32 = jnp.float32
BF16 = jnp.bfloat16

D_MODEL = 1024
CHUNK = 64
CHUNK_SHIFT = 6
N_MEM = 256
BRANCH_WIDTH = 512
N_BRANCH = 3
LRU_BLOCKS = 8
CONV_WIDTH = 4
LRU_C = 8.0
POOL_WINDOWS = (2, 4, 8, 16)
POOL_GROUP = BRANCH_WIDTH // len(POOL_WINDOWS)
ATT_HEADS = 8
HEAD_DIM = 64
IDX_HEADS = 8
IDX_DIM = 64
MAX_TOPK = 256
MEM_HEADS = 4
MEM_HEAD_DIM = 128
MEM_WIDTH = MEM_HEADS * MEM_HEAD_DIM
D_FF = 2816
RMS_EPS = 1e-6

LANES = 128
SUBLANES = 8
BF16_ROWS = 16
VMEM_LIMIT = 56 * 1024 * 1024

C_LX, C_LG, C_PIN = 0, 512, 1024
C_KK = 1536
C_GL = 1664
D_ROW = C_GL + N_BRANCH * D_MODEL
R_Q, R_QI, R_V, R_W = 0, 512, 1024, 1088
D_TR = R_W + BF16_ROWS
VT_ROWS = HEAD_DIM + BF16_ROWS
NEG_BIG = -1e30
F32_LOWEST = -3.4028234663852886e38
F32_MIN_NORMAL = 1.1754943508222875e-38


def _params(sem, vmem=VMEM_LIMIT):
    return pltpu.CompilerParams(dimension_semantics=sem, vmem_limit_bytes=vmem)


def _resident(shape):
    nd = len(shape)
    return pl.BlockSpec(shape, lambda *_: (0,) * nd, pipeline_mode=pl.Buffered(1))


def _rms(x, g):
    return x * lax.rsqrt(jnp.mean(x * x, axis=-1, keepdims=True) + RMS_EPS) * g


def _dot_nt(a, b):
    return lax.dot_general(a, b, (((1,), (1,)), ((), ())), preferred_element_type=F32)


CONV_PAD = 8
POOL_PAD = 16


def _log1p(x):
    u = 1.0 + x
    return jnp.where(u == 1.0, x, jnp.log(u) * (x / (u - 1.0)))


def _softplus(x):
    return jnp.maximum(x, 0.0) + _log1p(jnp.exp(-jnp.abs(x)))


def _scan_tile(a, u):
    ts = a.shape[0]
    row = lax.broadcasted_iota(jnp.int32, a.shape, 0)
    sh = 1
    while sh < ts:
        if sh % SUBLANES:
            keep = row >= sh
            a_sh = jnp.where(keep, pltpu.roll(a, sh, 0), 1.0)
            u_sh = jnp.where(keep, pltpu.roll(u, sh, 0), 0.0)
            u = a * u_sh + u
            a = a * a_sh
        else:
            u = jnp.concatenate([u[:sh], a[sh:] * u[:ts - sh] + u[sh:]], axis=0)
            a = jnp.concatenate([a[:sh], a[sh:] * a[:ts - sh]], axis=0)
        sh *= 2
    return a, u


def _lru_mixer(lx, lg, cw_ref, cb_ref, wa_ref, ba_ref, wx_ref, bx_ref, ap_ref, ya_ref,
               xbuf, hc_ref):
    ts = lx.shape[0]
    xbuf[CONV_PAD:CONV_PAD + ts, :] = lx
    ua = cb_ref[...] + cw_ref[CONV_WIDTH - 1:CONV_WIDTH, :] * lx
    for j in range(CONV_WIDTH - 1):
        off = CONV_PAD - (CONV_WIDTH - 1) + j
        ua = ua + cw_ref[j:j + 1, :] * xbuf[off:off + ts, :]
    xbuf[0:CONV_PAD, :] = xbuf[ts:ts + CONV_PAD, :]

    ub = ua.astype(BF16)
    r = jax.nn.sigmoid(jnp.dot(ub, wa_ref[...], preferred_element_type=F32) + ba_ref[...])
    ig = jax.nn.sigmoid(jnp.dot(ub, wx_ref[...], preferred_element_type=F32) + bx_ref[...])
    log_a = -LRU_C * r * _softplus(-ap_ref[...])
    a = jnp.exp(log_a)
    mult = jnp.sqrt(jnp.maximum(1.0 - jnp.exp(2.0 * log_a), 0.0))
    u = ua * ig * mult

    for c in range(BRANCH_WIDTH // LANES):
        cs = slice(c * LANES, (c + 1) * LANES)
        acum, hloc = _scan_tile(a[:, cs], u[:, cs])
        hfull = hloc + acum * hc_ref[:, cs]
        hc_ref[:, cs] = hfull[ts - 1:ts, :]
        ya_ref[:, cs] = (hfull * jax.nn.gelu(lg[:, cs], approximate=True)).astype(BF16)


def _pool_mixer(pin, s, wp_ref, ps_ref, yb_ref, pbuf):
    ts = pin.shape[0]
    pbuf[POOL_PAD:POOL_PAD + ts, :] = pin
    tpos = s * ts + lax.broadcasted_iota(jnp.int32, (ts, POOL_GROUP), 0)
    for gi, win in enumerate(POOL_WINDOWS):
        cs = slice(gi * POOL_GROUP, (gi + 1) * POOL_GROUP)
        cur = pin[:, cs]
        acc = cur
        for j in range(1, win):
            acc = acc + pbuf[POOL_PAD - j:POOL_PAD - j + ts, cs]
        count = jnp.minimum(tpos + 1, win).astype(F32)
        pooled = (acc / count - cur).astype(BF16)
        mixed = jnp.dot(pooled, wp_ref[gi], preferred_element_type=F32)
        yb_ref[:, cs] = (mixed * ps_ref[:, cs]).astype(BF16)
    pbuf[0:POOL_PAD, :] = pbuf[ts:ts + POOL_PAD, :]


def _mix_in_kernel(x_ref, g_ref, w_ref, wt_ref, cw_ref, cb_ref, wa_ref, ba_ref, wx_ref, bx_ref,
                   ap_ref, wp_ref, ps_ref, bg_ref,
                   ya_ref, yb_ref, k_ref, ki_ref, gate_ref, qt_ref, qit_ref, vt_ref, wit_ref,
                   xbuf, pbuf, hc_ref):
    s = pl.program_id(1)
    h = _rms(x_ref[...], g_ref[...]).astype(BF16)
    tm = h.shape[0]

    @pl.when(s == 0)
    def _():
        xbuf[0:CONV_PAD, :] = jnp.zeros((CONV_PAD, BRANCH_WIDTH), F32)
        pbuf[0:POOL_PAD, :] = jnp.zeros((POOL_PAD, BRANCH_WIDTH), F32)
        hc_ref[...] = jnp.zeros_like(hc_ref)

    def proj(lo, hi):
        return jnp.dot(h, w_ref[:, lo:hi], preferred_element_type=F32)

    lx, lg, pin = proj(C_LX, C_LG), proj(C_LG, C_PIN), proj(C_PIN, C_KK)
    kk = proj(C_KK, C_GL)
    k_ref[...] = kk[:, :HEAD_DIM].astype(BF16)
    ki_ref[...] = kk[:, HEAD_DIM:].astype(BF16)
    gate_ref[...] = jax.nn.sigmoid(proj(C_GL, D_ROW) + bg_ref[...]).astype(BF16)

    qt_ref[...] = (_dot_nt(wt_ref[R_Q:R_QI, :], h) * (HEAD_DIM ** -0.5 * math.log2(math.e))).astype(BF16)
    qit_ref[...] = (_dot_nt(wt_ref[R_QI:R_V, :], h) * (IDX_DIM ** -0.5)).astype(BF16)
    vt_ref[0:HEAD_DIM, :] = _dot_nt(wt_ref[R_V:R_W, :], h).astype(BF16)
    pad_row = lax.broadcasted_iota(jnp.int32, (BF16_ROWS, tm), 0)
    vt_ref[HEAD_DIM:VT_ROWS, :] = jnp.where(pad_row == 0, 1.0, 0.0).astype(BF16)
    wit_ref[...] = _dot_nt(wt_ref[R_W:D_TR, :], h)[:IDX_HEADS, :] * (IDX_HEADS ** -0.5)

    _lru_mixer(lx, lg, cw_ref, cb_ref, wa_ref, ba_ref, wx_ref, bx_ref, ap_ref, ya_ref,
               xbuf, hc_ref)
    _pool_mixer(pin, s, wp_ref, ps_ref, yb_ref, pbuf)


def _mix_in(xt, g, w, wt, cw, cb, wa, ba, wx, bx, ap, wp, ps, bg, bsz, seq, tm):
    t = xt.shape[0]
    nts = seq // tm
    nt = bsz * nts
    row = lambda n: pl.BlockSpec((tm, n), lambda b, s: (b * nts + s, 0))
    col = lambda r: pl.BlockSpec((r, tm), lambda b, s: (b * nts + s, 0))
    vec = _resident((1, BRANCH_WIDTH))
    row_outs = [(BRANCH_WIDTH, BF16), (BRANCH_WIDTH, BF16), (HEAD_DIM, BF16), (IDX_DIM, BF16),
                (N_BRANCH * D_MODEL, BF16)]
    out_specs = [row(n) for n, _ in row_outs] + [
        col(ATT_HEADS * HEAD_DIM), col(IDX_HEADS * IDX_DIM),
        pl.BlockSpec((VT_ROWS, tm), lambda b, s: (b, s)), col(IDX_HEADS)]
    out_shape = [jax.ShapeDtypeStruct((t, n), dt) for n, dt in row_outs] + [
        jax.ShapeDtypeStruct((nt * ATT_HEADS * HEAD_DIM, tm), BF16),
        jax.ShapeDtypeStruct((nt * IDX_HEADS * IDX_DIM, tm), BF16),
        jax.ShapeDtypeStruct((bsz * VT_ROWS, seq), BF16),
        jax.ShapeDtypeStruct((nt * IDX_HEADS, tm), F32)]
    return pl.pallas_call(
        _mix_in_kernel,
        grid=(bsz, nts),
        in_specs=[row(D_MODEL), _resident((1, D_MODEL)), _resident((D_MODEL, D_ROW)),
                  _resident((D_TR, D_MODEL)),
                  _resident((CONV_WIDTH, BRANCH_WIDTH)), vec,
                  _resident((BRANCH_WIDTH, BRANCH_WIDTH)), vec,
                  _resident((BRANCH_WIDTH, BRANCH_WIDTH)), vec, vec,
                  _resident((len(POOL_WINDOWS), POOL_GROUP, POOL_GROUP)), vec,
                  _resident((1, N_BRANCH * D_MODEL))],
        out_specs=out_specs,
        out_shape=out_shape,
        scratch_shapes=[pltpu.VMEM((CONV_PAD + tm, BRANCH_WIDTH), F32),
                        pltpu.VMEM((POOL_PAD + tm, BRANCH_WIDTH), F32),
                        pltpu.VMEM((1, BRANCH_WIDTH), F32)],
        compiler_params=_params(("arbitrary", "arbitrary")),
    )(xt, g, w, wt, cw, cb, wa, ba, wx, bx, ap, wp, ps, bg)


COUNT_ROWS = 64
REDUCE_ROWS = 32


def _key_to_float(key):
    bits = key ^ (lax.shift_right_arithmetic(key, 31) & jnp.int32(0x7FFFFFFF))
    return lax.bitcast_convert_type(bits, F32)


def _float_to_key(x):
    bits = lax.bitcast_convert_type(x, jnp.int32)
    return bits ^ (lax.shift_right_arithmetic(bits, 31) & jnp.int32(0x7FFFFFFF))


def _col_reduce(x, op):
    rows, n = x.shape
    return op(op(x.reshape(rows // REDUCE_ROWS, REDUCE_ROWS, n), axis=0), axis=0, keepdims=True)


def _dsa_kernel(qt_ref, qit_ref, wt_ref, k_ref, ki_ref, vt_ref, o_ref, sc_ref, scb_ref,
                *, tq, kc, seq, top_k):
    i = pl.program_id(1)
    nch = ((i + 1) * tq + kc - 1) // kc
    idx_bits = seq.bit_length() - 1

    lane = lax.broadcasted_iota(jnp.int32, (1, tq), 1)
    krow = lax.broadcasted_iota(jnp.int32, (kc, tq), 0)
    qpos = i * tq + lane
    lim = (lax.shift_right_logical(qpos, CHUNK_SHIFT) + 1) * CHUNK

    kca = 2 * kc

    def idx_step(off, rows):
        kic = ki_ref[pl.ds(off, rows), :]
        acc = None
        for h in range(IDX_HEADS):
            lt = jnp.dot(kic, qit_ref[h * IDX_DIM:(h + 1) * IDX_DIM, :], preferred_element_type=F32)
            term = jnp.maximum(lt, 0.0) * wt_ref[h:h + 1, :]
            acc = term if acc is None else acc + term
        kpos = off + lax.broadcasted_iota(jnp.int32, (rows, tq), 0)
        sc = jnp.where(kpos < lim, acc, -jnp.inf)
        sc_ref[pl.ds(off, rows), :] = sc
        scb_ref[pl.ds(off, rows), :] = sc.astype(jnp.bfloat16)

    def idx_body(c, carry):
        idx_step(pl.multiple_of(c * kca, kca), kca)
        return carry

    lax.fori_loop(0, nch // 2, idx_body, 0)
    tail_off = pl.multiple_of((nch - 1) * kc, kc)

    @pl.when(nch % 2 == 1)
    def _():
        idx_step(tail_off, kc)

    def count(pred):
        def body(c, cnt):
            off = pl.multiple_of(c * kc, kc)
            hit = jnp.where(pred(sc_ref[pl.ds(off, kc), :], off + krow), 1.0, 0.0)
            return cnt + jnp.sum(hit.reshape(kc // COUNT_ROWS, COUNT_ROWS, tq), axis=0)
        cnt = lax.fori_loop(0, nch, body, jnp.zeros((COUNT_ROWS, tq), F32))
        return jnp.sum(cnt, axis=0, keepdims=True)

    kf = jnp.float32(top_k)

    def count16(pb):
        def body(c, cnt):
            off = pl.multiple_of(c * kc, kc)
            hit = jnp.where(scb_ref[pl.ds(off, kc), :] >= pb, ONE16, ZERO16)
            for j in range(kc // COUNT_ROWS):
                cnt = cnt + hit[j * COUNT_ROWS:(j + 1) * COUNT_ROWS]
            return cnt
        cnt = lax.fori_loop(0, nch, body, jnp.zeros((COUNT_ROWS, tq), jnp.bfloat16))
        return jnp.sum(cnt.astype(F32), axis=0, keepdims=True)

    ONE16, ZERO16 = jnp.ones((), jnp.bfloat16), jnp.zeros((), jnp.bfloat16)

    def bit16_body(it, cur):
        cand = cur + lax.shift_left(jnp.int32(1), 15 - it)
        half = cand ^ (lax.shift_right_arithmetic(cand, 31) & jnp.int32(0x7FFF))
        pb = lax.bitcast_convert_type(lax.shift_left(half, 16), F32).astype(jnp.bfloat16)
        return jnp.where(count16(pb) >= kf, cand, cur)

    cur16 = lax.fori_loop(0, 16, bit16_body, jnp.full((1, tq), -2 ** 15, jnp.int32))
    half = cur16 ^ (lax.shift_right_arithmetic(cur16, 31) & jnp.int32(0x7FFF))
    bucket_key = _float_to_key(lax.bitcast_convert_type(lax.shift_left(half, 16), F32))

    top = bucket_key + (2 ** 15 + 2)

    def bit_body(it, carry):
        o, c_rej = carry
        cand = o + lax.shift_left(jnp.int32(1), 16 - it)
        pf = _key_to_float(top - cand)
        cnt = count(lambda x, _: x >= pf)
        below = cnt < kf
        return jnp.where(below, cand, o), jnp.where(below, c_rej, cnt)

    o_max, c_ge = lax.fori_loop(0, 17, bit_body, (jnp.zeros((1, tq), jnp.int32),
                                                  jnp.zeros((1, tq), F32)))
    cur = top - o_max - 1
    thr = _key_to_float(cur)
    nxt = _key_to_float(cur + 1)
    tiny = jnp.abs(thr) < F32_MIN_NORMAL
    thr = jnp.where(tiny, 0.0, thr)
    nxt = jnp.where(tiny, F32_MIN_NORMAL, nxt)
    sel_all = lim <= top_k
    excess = jnp.where(jnp.logical_and(c_ge > kf, jnp.logical_not(sel_all)), 1.0, 0.0)

    def tie_limit():
        need = kf - count(lambda x, _: x > thr)

        def jbit(it, jc):
            cand = jc + lax.shift_left(jnp.int32(1), idx_bits - 1 - it)
            cnt = count(lambda x, kp: jnp.logical_and(x == thr, kp < cand))
            return jnp.where(cnt < need, cand, jc)
        return lax.fori_loop(0, idx_bits, jbit, jnp.zeros((1, tq), jnp.int32))

    jlim = lax.cond(jnp.max(excess) > 0.0, tie_limit,
                    lambda: jnp.full((1, tq), seq, jnp.int32))
    jlim = jnp.where(sel_all, seq, jlim)
    piv_lo = jnp.where(sel_all, F32_LOWEST, thr)
    piv_hi = jnp.where(sel_all, F32_LOWEST, nxt)

    def att_step(off, rows, carry):
        m, accs = carry
        kpos = off + lax.broadcasted_iota(jnp.int32, (rows, tq), 0)
        pivot = jnp.where(kpos <= jlim, piv_lo, piv_hi)
        bias = jnp.where(sc_ref[pl.ds(off, rows), :] >= pivot, 0.0, -jnp.inf)
        kch = k_ref[pl.ds(off, rows), :]
        vch = vt_ref[:, pl.ds(off, rows)]
        ss = [jnp.dot(kch, qt_ref[h * HEAD_DIM:(h + 1) * HEAD_DIM, :],
                      preferred_element_type=F32) + bias for h in range(ATT_HEADS)]
        m_new = jnp.maximum(m, jnp.concatenate([_col_reduce(s, jnp.max) for s in ss], axis=0))
        alpha = jnp.exp2(m - m_new)
        new_accs = []
        for h in range(ATT_HEADS):
            p = jnp.exp2(ss[h] - m_new[h:h + 1, :]).astype(BF16)
            new_accs.append(alpha[h:h + 1, :] * accs[h]
                            + jnp.dot(vch, p, preferred_element_type=F32))
        return m_new, tuple(new_accs)

    init = (jnp.full((ATT_HEADS, tq), NEG_BIG, F32),
            tuple(jnp.zeros((VT_ROWS, tq), F32) for _ in range(ATT_HEADS)))
    carry = lax.fori_loop(0, nch // 2,
                          lambda c, cr: att_step(pl.multiple_of(c * kca, kca), kca, cr), init)
    _, accs = lax.cond(nch % 2 == 1, lambda cr: att_step(tail_off, kc, cr), lambda cr: cr, carry)
    for h in range(ATT_HEADS):
        o_ref[h * HEAD_DIM:(h + 1) * HEAD_DIM, :] = (
            accs[h][0:HEAD_DIM, :] / accs[h][HEAD_DIM:HEAD_DIM + 1, :])


def _dsa(qt, qit, wit, k, ki, vt, bsz, seq, tq, kc):
    nt = seq // tq
    top_k = min(MAX_TOPK, seq // 4)
    assert seq // COUNT_ROWS <= 256
    tile = lambda r: pl.BlockSpec((r, tq), lambda b, i: (b * nt + i, 0))
    per_batch = lambda r, n: pl.BlockSpec((r, n), lambda b, i: (b, 0))
    return pl.pallas_call(
        functools.partial(_dsa_kernel, tq=tq, kc=kc, seq=seq, top_k=top_k),
        grid=(bsz, nt),
        in_specs=[tile(ATT_HEADS * HEAD_DIM), tile(IDX_HEADS * IDX_DIM), tile(IDX_HEADS),
                  per_batch(seq, HEAD_DIM), per_batch(seq, IDX_DIM), per_batch(VT_ROWS, seq)],
        out_specs=tile(ATT_HEADS * HEAD_DIM),
        out_shape=jax.ShapeDtypeStruct((bsz * nt * ATT_HEADS * HEAD_DIM, tq), F32),
        scratch_shapes=[pltpu.VMEM((seq, tq), F32), pltpu.VMEM((seq, tq), jnp.bfloat16)],
        compiler_params=_params(("parallel", "arbitrary")),
    )(qt, qit, wit, k, ki, vt)


FF_CHUNK = 256


def _mem_kv_kernel(m_ref, g_ref, w_ref, k_ref, v_ref):
    mn = _rms(m_ref[...], g_ref[...]).astype(BF16)
    k_ref[...] = jnp.dot(mn, w_ref[:, :MEM_WIDTH], preferred_element_type=F32).astype(BF16)
    v_ref[...] = jnp.dot(mn, w_ref[:, MEM_WIDTH:], preferred_element_type=F32).astype(BF16)


def _mem_kv(memt, g, w, bsz):
    blk = lambda n: pl.BlockSpec((N_MEM, n), lambda b: (b, 0))
    return pl.pallas_call(
        _mem_kv_kernel,
        grid=(bsz,),
        in_specs=[blk(D_MODEL), _resident((1, D_MODEL)), _resident((D_MODEL, 2 * MEM_WIDTH))],
        out_specs=[blk(MEM_WIDTH), blk(MEM_WIDTH)],
        out_shape=[jax.ShapeDtypeStruct((bsz * N_MEM, MEM_WIDTH), BF16)] * 2,
        compiler_params=_params(("parallel",)),
    )(memt, g, w)


def _post_kernel(x_ref, ya_ref, yb_ref, yct_ref, gate_ref, wb_ref, wo_ref, gmix_ref,
                 km_ref, vm_ref, gmp_ref, wq_ref, wmo_ref, gmo_ref,
                 gfp_ref, wfi_ref, wfo_ref, gfo_ref, o_ref, acc_ref):
    nf = ATT_HEADS * HEAD_DIM
    yc = jnp.concatenate([yct_ref[j * nf:(j + 1) * nf, :].T
                          for j in range(yct_ref.shape[0] // nf)], axis=0).astype(BF16)
    branches = (ya_ref[...], yb_ref[...], yc)
    merged = None
    for n, y in enumerate(branches):
        cs = slice(n * D_MODEL, (n + 1) * D_MODEL)
        term = gate_ref[:, cs] * jnp.dot(y, wb_ref[n], preferred_element_type=F32)
        merged = term if merged is None else merged + term
    out = jnp.dot(merged.astype(BF16), wo_ref[...], preferred_element_type=F32)
    x = x_ref[...] + _rms(out, gmix_ref[...])

    h = _rms(x, gmp_ref[...]).astype(BF16)
    q = jnp.dot(h, wq_ref[...], preferred_element_type=F32).astype(BF16)
    heads = []
    for hd in range(MEM_HEADS):
        cs = slice(hd * MEM_HEAD_DIM, (hd + 1) * MEM_HEAD_DIM)
        att = _dot_nt(q[:, cs], km_ref[:, cs]) * (MEM_HEAD_DIM ** -0.5)
        p = jnp.exp(att - jnp.max(att, axis=-1, keepdims=True))
        prob = (p / jnp.sum(p, axis=-1, keepdims=True)).astype(BF16)
        heads.append(jnp.dot(prob, vm_ref[:, cs], preferred_element_type=F32).astype(BF16))
    out = jnp.dot(jnp.concatenate(heads, axis=-1), wmo_ref[...], preferred_element_type=F32)
    x = x + _rms(out, gmo_ref[...])

    h = _rms(x, gfp_ref[...]).astype(BF16)
    for c in range(D_FF // FF_CHUNK):
        lo = c * FF_CHUNK
        fg = jnp.dot(h, wfi_ref[:, lo:lo + FF_CHUNK], preferred_element_type=F32)
        fu = jnp.dot(h, wfi_ref[:, D_FF + lo:D_FF + lo + FF_CHUNK], preferred_element_type=F32)
        act = (fg * jax.nn.sigmoid(fg) * fu).astype(BF16)
        part = jnp.dot(act, wfo_ref[lo:lo + FF_CHUNK, :], preferred_element_type=F32)
        if c == 0:
            acc_ref[...] = part
        else:
            acc_ref[...] += part
    o_ref[...] = x + _rms(acc_ref[...], gfo_ref[...])


def _post(xt, ya, yb, yct, gates, wb, wo, gmix, kmem, vmem, gmp, wq, wmo, gmo,
          gfp, wfi, wfo, gfo, bsz, seq, tm, tq):
    nts = seq // tm
    row = lambda n: pl.BlockSpec((tm, n), lambda b, i: (b * nts + i, 0))
    memblk = pl.BlockSpec((N_MEM, MEM_WIDTH), lambda b, i: (b, 0))
    gvec = _resident((1, D_MODEL))
    return pl.pallas_call(
        _post_kernel,
        grid=(bsz, nts),
        in_specs=[row(D_MODEL), row(BRANCH_WIDTH), row(BRANCH_WIDTH),
                  pl.BlockSpec((tm // tq * ATT_HEADS * HEAD_DIM, tq), lambda b, i: (b * nts + i, 0)),
                  row(N_BRANCH * D_MODEL),
                  _resident((N_BRANCH, BRANCH_WIDTH, D_MODEL)), _resident((D_MODEL, D_MODEL)), gvec,
                  memblk, memblk, gvec, _resident((D_MODEL, MEM_WIDTH)),
                  _resident((MEM_WIDTH, D_MODEL)), gvec,
                  gvec, _resident((D_MODEL, 2 * D_FF)), _resident((D_FF, D_MODEL)), gvec],
        out_specs=row(D_MODEL),
        out_shape=jax.ShapeDtypeStruct(xt.shape, F32),
        scratch_shapes=[pltpu.VMEM((tm, D_MODEL), F32)],
        input_output_aliases={0: 0},
        compiler_params=_params(("parallel", "parallel")),
    )(xt, ya, yb, yct, gates, wb, wo, gmix, kmem, vmem, gmp, wq, wmo, gmo, gfp, wfi, wfo, gfo)


def _pack_w_in(w):
    o_q = 3 * BRANCH_WIDTH
    o_k = o_q + ATT_HEADS * HEAD_DIM
    o_v = o_k + HEAD_DIM
    o_qi = o_v + HEAD_DIM
    o_ki = o_qi + IDX_HEADS * IDX_DIM
    o_wi = o_ki + IDX_DIM
    o_gl = o_wi + IDX_HEADS
    w_row = jnp.concatenate([w[:, :o_q], w[:, o_k:o_v], w[:, o_ki:o_wi], w[:, o_gl:]], axis=1)
    w_tr = jnp.concatenate([w[:, o_q:o_k], w[:, o_qi:o_ki], w[:, o_v:o_qi], w[:, o_wi:o_gl],
                            jnp.zeros((D_MODEL, D_TR - R_W - IDX_HEADS), w.dtype)], axis=1).T
    return w_row.astype(BF16), w_tr.astype(BF16)


def _block_diag(w):
    nb, n, _ = w.shape
    eye = jnp.eye(nb, dtype=w.dtype)
    return (eye[:, None, :, None] * w[:, :, None, :]).reshape(nb * n, nb * n)


def kernel(x, mem, g_mix_pre, w_in, conv_w, conv_b, lru_w_a, lru_b_a, lru_w_x, lru_b_x,
           lru_a_param, w_pool, pool_scale, w_branch, b_gate, w_out, g_mix_post,
           g_mem_pre, g_mem_kv, w_mem_q, w_mem_kv, w_mem_o, g_mem_post,
           g_ffn_pre, w_ffn_in, w_ffn_out, g_ffn_post):
    bsz, seq, d = x.shape
    assert d == D_MODEL and seq % 512 == 0 and (seq & (seq - 1)) == 0
    t = bsz * seq
    tm = 256

    vec = lambda a: a.reshape(a.shape[0], 1, -1)
    layers = dict(
        g_mix_pre=vec(g_mix_pre), w_in=w_in, conv_w=conv_w, conv_b=vec(conv_b),
        lru_w_a=lru_w_a, lru_b_a=vec(lru_b_a), lru_w_x=lru_w_x, lru_b_x=vec(lru_b_x),
        lru_a_param=vec(lru_a_param), w_pool=w_pool, pool_scale=vec(pool_scale),
        w_branch=w_branch, b_gate=b_gate.reshape(b_gate.shape[0], 1, -1), w_out=w_out,
        g_mix_post=vec(g_mix_post), g_mem_pre=vec(g_mem_pre), g_mem_kv=vec(g_mem_kv),
        w_mem_q=w_mem_q, w_mem_kv=w_mem_kv, w_mem_o=w_mem_o, g_mem_post=vec(g_mem_post),
        g_ffn_pre=vec(g_ffn_pre), w_ffn_in=w_ffn_in, w_ffn_out=w_ffn_out,
        g_ffn_post=vec(g_ffn_post))
    memt = mem.reshape(bsz * N_MEM, D_MODEL)

    def layer(xt, p):
        w_row, w_tr = _pack_w_in(p["w_in"])
        ya, yb, k, ki, gates, qt, qit, vt, wit = _mix_in(
            xt, p["g_mix_pre"], w_row, w_tr, p["conv_w"], p["conv_b"],
            _block_diag(p["lru_w_a"]).astype(BF16), p["lru_b_a"],
            _block_diag(p["lru_w_x"]).astype(BF16), p["lru_b_x"],
            p["lru_a_param"], p["w_pool"].astype(BF16), p["pool_scale"], p["b_gate"],
            bsz, seq, tm)
        yct = _dsa(qt, qit, wit, k, ki, vt, bsz, seq, tm, tm)
        kmem, vmem = _mem_kv(memt, p["g_mem_kv"], p["w_mem_kv"].astype(BF16), bsz)
        xt = _post(xt, ya, yb, yct, gates, p["w_branch"].astype(BF16), p["w_out"].astype(BF16),
                   p["g_mix_post"], kmem, vmem, p["g_mem_pre"], p["w_mem_q"].astype(BF16),
                   p["w_mem_o"].astype(BF16), p["g_mem_post"], p["g_ffn_pre"],
                   p["w_ffn_in"].astype(BF16), p["w_ffn_out"].astype(BF16), p["g_ffn_post"],
                   bsz, seq, 2 * tm, tm)
        return xt, None

    xt, _ = lax.scan(layer, x.reshape(t, D_MODEL), layers)
    return xt.reshape(bsz, seq, D_MODEL)
```
